```python
import math
import jax, jax.numpy as jnp
from jax import lax
import numpy as np

D_MODEL = 1024
BATCH = 4
SEQ = 4096
DEPTH = 2
DEC_BATCH = 128
DEC_SEQ = 8
PAST_LEN = 2048
PAGE_SIZE = 128

RET_HEADS = 4
RET_DK = 128
RET_DV = 128
RET_CHUNK = 128
RET_THETA = 10000.0
ATT_HEADS = 8
ATT_KV_HEADS = 2
ATT_HEAD_DIM = 64
ATT_GROUP = ATT_HEADS // ATT_KV_HEADS
ROPE_THETA = 500000.0
ROPE_DIM = ATT_HEAD_DIM // 4
IDX_HEADS = 4
IDX_DIM = 64
IDX_ROPE_DIM = IDX_DIM // 4
TOPK_MAX = 256
Q_BLOCK = 128
PLE_DIM = 256
NORM_EPS = 1e-6
GN_EPS = 1e-5

SPLIT_SIZES = (RET_HEADS * RET_DK, RET_HEADS * RET_DK, RET_HEADS * RET_DV, RET_HEADS * RET_DV,
               ATT_HEADS * ATT_HEAD_DIM, ATT_KV_HEADS * ATT_HEAD_DIM, ATT_KV_HEADS * ATT_HEAD_DIM,
               ATT_HEADS * ATT_HEAD_DIM, IDX_HEADS * IDX_DIM, IDX_DIM, IDX_HEADS, D_MODEL, D_MODEL)
IN_WIDTH = sum(SPLIT_SIZES)

kernel_name = "hybrid_retention_dsa_decode_step"


def rmsnorm(x, g):
    xf = x.astype(jnp.float32)
    r = xf * lax.rsqrt(jnp.mean(xf * xf, axis=-1, keepdims=True) + NORM_EPS)
    return (r * g.astype(jnp.float32)).astype(x.dtype)


def rope(x, pos, rot_dim, theta):
    half = rot_dim // 2
    freqs = jnp.exp(-math.log(theta) * jnp.arange(half, dtype=jnp.float32) / half)
    ang = pos.astype(jnp.float32)[:, None] * freqs[None, :]
    cos = jnp.cos(ang)[None, :, None, :]
    sin = jnp.sin(ang)[None, :, None, :]
    xf = x.astype(jnp.float32)
    x1 = xf[..., :half]
    x2 = xf[..., half:rot_dim]
    out = jnp.concatenate([x1 * cos - x2 * sin, x2 * cos + x1 * sin, xf[..., rot_dim:]], axis=-1)
    return out.astype(x.dtype)


def in_projection(h, w_in, pos, q_gain, k_gain):
    B, T, _ = h.shape
    z = jnp.einsum('btd,de->bte', h, w_in)
    offsets = np.cumsum(np.array(SPLIT_SIZES))[:-1].tolist()
    rq, rk, rv, rz, aq, ak, av, az, iq, ik, iw, gr, ga = jnp.split(z, offsets, axis=-1)
    rq = rope(rq.reshape(B, T, RET_HEADS, RET_DK), pos, RET_DK, RET_THETA)
    rk = rope(rk.reshape(B, T, RET_HEADS, RET_DK), pos, RET_DK, RET_THETA) * (RET_DK ** -0.5)
    rv = rv.reshape(B, T, RET_HEADS, RET_DV)
    aq = rope(rmsnorm(aq.reshape(B, T, ATT_HEADS, ATT_HEAD_DIM), q_gain), pos, ROPE_DIM, ROPE_THETA)
    ak = rope(rmsnorm(ak.reshape(B, T, ATT_KV_HEADS, ATT_HEAD_DIM), k_gain), pos, ROPE_DIM, ROPE_THETA)
    av = av.reshape(B, T, ATT_KV_HEADS, ATT_HEAD_DIM)
    iq = rope(iq.reshape(B, T, IDX_HEADS, IDX_DIM), pos, IDX_ROPE_DIM, ROPE_THETA)
    ik = rope(ik.reshape(B, T, 1, IDX_DIM), pos, IDX_ROPE_DIM, ROPE_THETA)[:, :, 0]
    return rq, rk, rv, rz, aq, ak, av, az, iq, ik, iw, gr, ga


def retention(q, k, v, s0):
    B, T, H, Dk = q.shape
    Dv = v.shape[-1]
    C = RET_CHUNK if T % RET_CHUNK == 0 else T
    n = T // C
    log_g = jnp.log1p(-jnp.exp2(-5.0 - jnp.arange(H, dtype=jnp.float32)))
    c = jnp.arange(C, dtype=jnp.float32)
    diff = c[:, None] - c[None, :]
    decay_intra = jnp.where(diff[None] >= 0, jnp.exp(jnp.maximum(diff, 0.0)[None] * log_g[:, None, None]), 0.0)
    decay_q = jnp.exp((c[:, None] + 1.0) * log_g[None, :])
    decay_k = jnp.exp((C - 1.0 - c)[:, None] * log_g[None, :])
    decay_s = jnp.exp(C * log_g)

    def to_chunks(a):
        return a.astype(jnp.float32).reshape(B, n, C, H, a.shape[-1]).transpose(1, 0, 2, 3, 4)

    def step(S, inp):
        qc, kc, vc = inp
        inner = jnp.einsum('bchd,bshd->bhcs', qc, kc) * decay_intra[None]
        o = (jnp.einsum('bhcs,bshe->bche', inner, vc)
             + jnp.einsum('bchd,bhde->bche', qc * decay_q[None, :, :, None], S))
        S = S * decay_s[None, :, None, None] + jnp.einsum('bchd,bche->bhde', kc * decay_k[None, :, :, None], vc)
        return S, o

    S, o = lax.scan(step, s0.astype(jnp.float32), (to_chunks(q), to_chunks(k), to_chunks(v)))
    o = o.transpose(1, 0, 2, 3, 4).reshape(B, T, H, Dv)
    return o, S


def sparse_attention(q, qpos, k_all, v_all, iq, iw, ik_all, topk):
    B, Tq = q.shape[:2]
    L = k_all.shape[1]
    causal = jnp.arange(L)[None, :] <= qpos[:, None]
    logits = jnp.einsum('bthd,bsd->bths', iq.astype(jnp.float32), ik_all.astype(jnp.float32)) * (IDX_DIM ** -0.5)
    score = jnp.einsum('bths,bth->bts', jax.nn.relu(logits), iw.astype(jnp.float32) * (IDX_HEADS ** -0.5))
    score = jnp.where(causal[None], score, -jnp.inf)
    _, sel = lax.top_k(score, topk)
    valid = sel <= qpos[None, :, None]
    kg = jax.vmap(lambda kb, ib: kb[ib])(k_all, sel)
    vg = jax.vmap(lambda vb, ib: vb[ib])(v_all, sel)
    qg = q.reshape(B, Tq, ATT_KV_HEADS, ATT_GROUP, ATT_HEAD_DIM).astype(jnp.float32)
    att = jnp.einsum('btkgd,btjkd->btkgj', qg, kg.astype(jnp.float32)) * (ATT_HEAD_DIM ** -0.5)
    att = jnp.where(valid[:, :, None, None, :], att, -jnp.inf)
    p = jax.nn.softmax(att, axis=-1)
    o = jnp.einsum('btkgj,btjkd->btkgd', p, vg.astype(jnp.float32))
    return o.reshape(B, Tq, ATT_HEADS * ATT_HEAD_DIM).astype(q.dtype)


def prompt_sparse_attention(aq, iq, iw, ak, av, ik, topk):
    B, T = aq.shape[:2]
    nb = T // Q_BLOCK

    def blk(a):
        return a.reshape(B, nb, Q_BLOCK, *a.shape[2:]).swapaxes(0, 1)

    pos_b = jnp.arange(T, dtype=jnp.int32).reshape(nb, Q_BLOCK)
    out = lax.map(lambda xs: sparse_attention(xs[0], xs[1], ak, av, xs[2], xs[3], ik, topk),
                  (blk(aq), pos_b, blk(iq), blk(iw)))
    return out.swapaxes(0, 1).reshape(B, T, -1)


def gather_pages(pool, page_table):
    g = pool[page_table]
    return g.reshape(page_table.shape[0], -1, *pool.shape[2:])


def group_norm_heads(o):
    mu = jnp.mean(o, axis=-1, keepdims=True)
    var = jnp.mean(jnp.square(o - mu), axis=-1, keepdims=True)
    return (o - mu) * lax.rsqrt(var + GN_EPS)


def decoder_layer(x, p_i, pos, ret_s0, past, topk, g_norm, w_in, q_gain, k_gain,
                  w_o_ret, w_o_att, w_out, w_ple_gate, w_ple_proj):
    B, T, _ = x.shape
    h = rmsnorm(x, g_norm)
    rq, rk, rv, rz, aq, ak, av, az, iq, ik, iw, gr, ga = in_projection(h, w_in, pos, q_gain, k_gain)
    o_r, s_new = retention(rq, rk, rv, ret_s0)
    o_r = group_norm_heads(o_r).reshape(B, T, RET_HEADS * RET_DV).astype(x.dtype)
    u_r = jnp.einsum('bte,ed->btd', o_r * jax.nn.silu(rz), w_o_ret)
    if past is None:
        o_a = prompt_sparse_attention(aq, iq, iw, ak, av, ik, topk)
    else:
        pk, pv, pik = past
        k_all = jnp.concatenate([pk.astype(ak.dtype), ak], axis=1)
        v_all = jnp.concatenate([pv.astype(av.dtype), av], axis=1)
        ik_all = jnp.concatenate([pik.astype(ik.dtype), ik], axis=1)
        o_a = sparse_attention(aq, pos, k_all, v_all, iq, iw, ik_all, topk)
    u_a = jnp.einsum('bte,ed->btd', o_a * jax.nn.silu(az), w_o_att)
    m = jax.nn.sigmoid(gr) * u_r + jax.nn.sigmoid(ga) * u_a
    x = x + jnp.einsum('btd,de->bte', m, w_out)
    gate = jax.nn.sigmoid(jnp.einsum('btd,de->bte', x, w_ple_gate))
    x = x + gate * jnp.einsum('btp,pd->btd', p_i, w_ple_proj)
    return x, ak, av, ik, s_new


def setup_inputs(seed: int = 0) -> dict:
    key = jax.random.key(seed)
    ks = jax.random.split(key, 20)
    n_pages = PAST_LEN // PAGE_SIZE
    n_used = DEC_BATCH * n_pages
    n_pool = n_used + n_used // 4
    f32 = jnp.float32
    nrm = lambda k, s, sc: jax.random.normal(k, s, f32) * sc
    x_prompt = nrm(ks[0], (BATCH, SEQ, D_MODEL), 1.0)
    x_sample = nrm(ks[1], (DEC_BATCH, DEC_SEQ, D_MODEL), 1.0)
    cache_k = nrm(ks[2], (DEPTH, n_pool, PAGE_SIZE, ATT_KV_HEADS, ATT_HEAD_DIM), 1.0)
    cache_v = nrm(ks[3], (DEPTH, n_pool, PAGE_SIZE, ATT_KV_HEADS, ATT_HEAD_DIM), 1.0)
    cache_idx_k = nrm(ks[4], (DEPTH, n_pool, PAGE_SIZE, IDX_DIM), 1.0)
    state_ret = nrm(ks[5], (DEPTH, DEC_BATCH, RET_HEADS, RET_DK, RET_DV), 0.5)
    perm = jax.random.permutation(ks[6], n_pool)[:n_used]
    page_table = perm.reshape(DEC_BATCH, n_pages).astype(jnp.int32)
    p_prompt = nrm(ks[7], (DEPTH, BATCH, SEQ, PLE_DIM), 1.0)
    p_sample = nrm(ks[8], (DEPTH, DEC_BATCH, DEC_SEQ, PLE_DIM), 1.0)
    norm_gain = 1.0 + nrm(ks[9], (DEPTH, D_MODEL), 0.1)
    w_in = nrm(ks[10], (DEPTH, D_MODEL, IN_WIDTH), D_MODEL ** -0.5)
    q_norm_gain = 1.0 + nrm(ks[11], (DEPTH, ATT_HEAD_DIM), 0.1)
    k_norm_gain = 1.0 + nrm(ks[12], (DEPTH, ATT_HEAD_DIM), 0.1)
    w_o_ret = nrm(ks[13], (DEPTH, RET_HEADS * RET_DV, D_MODEL), (RET_HEADS * RET_DV) ** -0.5)
    w_o_att = nrm(ks[14], (DEPTH, ATT_HEADS * ATT_HEAD_DIM, D_MODEL), (ATT_HEADS * ATT_HEAD_DIM) ** -0.5)
    w_out = nrm(ks[15], (DEPTH, D_MODEL, D_MODEL), D_MODEL ** -0.5)
    w_ple_gate = nrm(ks[16], (DEPTH, D_MODEL, D_MODEL), D_MODEL ** -0.5)
    w_ple_proj = nrm(ks[17], (DEPTH, PLE_DIM, D_MODEL), PLE_DIM ** -0.5)
    return {"x_prompt": x_prompt, "x_sample": x_sample, "cache_k": cache_k, "cache_v": cache_v,
            "cache_idx_k": cache_idx_k, "state_ret": state_ret, "page_table": page_table,
            "p_prompt": p_prompt, "p_sample": p_sample, "norm_gain": norm_gain, "w_in": w_in,
            "q_norm_gain": q_norm_gain, "k_norm_gain": k_norm_gain, "w_o_ret": w_o_ret,
            "w_o_att": w_o_att, "w_out": w_out, "w_ple_gate": w_ple_gate, "w_ple_proj": w_ple_proj}


def reference(x_prompt, x_sample, cache_k, cache_v, cache_idx_k, state_ret, page_table,
              p_prompt, p_sample, norm_gain, w_in, q_norm_gain, k_norm_gain, w_o_ret,
              w_o_att, w_out, w_ple_gate, w_ple_proj):
    Bp, Tp, _ = x_prompt.shape
    Bs, Ts, _ = x_sample.shape
    past_len = page_table.shape[1] * cache_k.shape[2]
    topk_prompt = min(TOPK_MAX, Tp // 4)
    topk_sample = min(TOPK_MAX, (past_len + Ts) // 4)
    pos_prompt = jnp.arange(Tp, dtype=jnp.int32)
    pos_sample = past_len + jnp.arange(Ts, dtype=jnp.int32)
    s0_prompt = jnp.zeros((Bp, RET_HEADS, RET_DK, RET_DV), jnp.float32)

    xp = x_prompt
    kp_l, vp_l, ikp_l, sp_l = [], [], [], []
    for i in range(DEPTH):
        xp, ak, av, ik, s_new = decoder_layer(
            xp, p_prompt[i], pos_prompt, s0_prompt, None, topk_prompt, norm_gain[i], w_in[i],
            q_norm_gain[i], k_norm_gain[i], w_o_ret[i], w_o_att[i], w_out[i], w_ple_gate[i], w_ple_proj[i])
        kp_l.append(ak); vp_l.append(av); ikp_l.append(ik); sp_l.append(s_new)

    xs = x_sample
    ks_l, vs_l, iks_l, ss_l = [], [], [], []
    for i in range(DEPTH):
        past = (gather_pages(cache_k[i], page_table), gather_pages(cache_v[i], page_table),
                gather_pages(cache_idx_k[i], page_table))
        xs, ak, av, ik, s_new = decoder_layer(
            xs, p_sample[i], pos_sample, state_ret[i], past, topk_sample, norm_gain[i], w_in[i],
            q_norm_gain[i], k_norm_gain[i], w_o_ret[i], w_o_att[i], w_out[i], w_ple_gate[i], w_ple_proj[i])
        ks_l.append(ak); vs_l.append(av); iks_l.append(ik); ss_l.append(s_new)

    k_prompt = jnp.stack(kp_l)
    v_prompt = jnp.stack(vp_l)
    idx_k_prompt = jnp.stack(ikp_l)
    ret_state_prompt = jnp.stack(sp_l)
    k_sample = jnp.stack(ks_l)
    v_sample = jnp.stack(vs_l)
    idx_k_sample = jnp.stack(iks_l)
    ret_state_sample = jnp.stack(ss_l)
    return (xp, xs, k_prompt, v_prompt, idx_k_prompt, ret_state_prompt,
            k_sample, v_sample, idx_k_sample, ret_state_sample)
```

```python
import functools
import math

import jax
import jax.numpy as jnp
from jax import lax
from jax.experimental import pallas as pl
from jax.experimental.pallas import tpu as pltpu

F32 = jnp.float32
BF16 = jnp.bfloat16
I32 = jnp.int32

RET_HEADS = 4
RET_DK = 128
RET_DV = 128
RET_CHUNK = 128
RET_THETA = 10000.0
ATT_HEADS = 8
ATT_KV_HEADS = 2
ATT_HEAD_DIM = 64
ROPE_THETA = 500000.0
ROPE_DIM = ATT_HEAD_DIM // 4
IDX_HEADS = 4
IDX_DIM = 64
TOPK_MAX = 256
NORM_EPS = 1e-6
GN_EPS = 1e-5

LANES = 128
Q_TILE = 128
KEY_CHUNK = 512
VMEM_LIMIT = 56 * 1024 * 1024
NEG_BIG = -1e30
INT_MIN = -(2 ** 31)

C_RQ, C_RK, C_RV, C_RZ = 0, 512, 1024, 1536
C_AQ, C_AZ, C_AK, C_AV = 2048, 2560, 3072, 3200
C_IQ, C_IKK, C_IW, C_GR, C_GA = 3328, 3584, 3712, 3840, 4864
W_PACKED = 5888

NT_DIMS = (((1,), (1,)), ((), ()))


def _silu(x):
    return x * jax.nn.sigmoid(x)


def _inproj_kernel(x_ref, g_ref, w_ref, rc_ref, rs_ref, ac_ref, as1_ref, as2_ref, qg_ref, kg_ref,
                   gsum_ref,
                   rq_o, rk_o, rv_o, rz_o, aq_o, az_o, akb_o, avb_o, iq_o, ikk_o, gr_o, ga_o,
                   ak_o, av_o, ik_o, iw_o):
    sd = rq_o.dtype
    x = x_ref[...]
    ms = jnp.mean(x * x, axis=-1, keepdims=True)
    hb = ((x * lax.rsqrt(ms + NORM_EPS)) * g_ref[...]).astype(BF16)

    def mm(c0, width):
        return jnp.dot(hb, w_ref[:, c0:c0 + width], preferred_element_type=F32)

    rc, rs = rc_ref[...], rs_ref[...]
    ac, as1, as2 = ac_ref[...], as1_ref[...], as2_ref[...]
    gsum = gsum_ref[...]

    def rope_ret(z):
        return z * rc + pltpu.roll(z, 64, 1) * rs

    def rope_att(z):
        return z * ac + pltpu.roll(z, LANES - 8, 1) * as1 + pltpu.roll(z, 8, 1) * as2

    def head_norm(z, gain):
        sq = z * z
        hi = sq.astype(BF16)
        lo = (sq - hi.astype(F32)).astype(BF16)
        ssq = (jnp.dot(hi, gsum, preferred_element_type=F32)
               + jnp.dot(lo, gsum, preferred_element_type=F32))
        return (z * lax.rsqrt(ssq * (1.0 / ATT_HEAD_DIM) + NORM_EPS)) * gain

    def slab(z, s):
        return z[:, s * LANES:(s + 1) * LANES]

    z = mm(C_RQ, 512)
    for s in range(4):
        rq_o[:, s * LANES:(s + 1) * LANES] = rope_ret(slab(z, s)).astype(sd)
    z = mm(C_RK, 512)
    for s in range(4):
        rk_o[:, s * LANES:(s + 1) * LANES] = (rope_ret(slab(z, s)) * (RET_DK ** -0.5)).astype(sd)
    rv_o[...] = mm(C_RV, 512).astype(sd)
    rz_o[...] = mm(C_RZ, 512).astype(sd)

    qg, kg = qg_ref[...], kg_ref[...]
    z = mm(C_AQ, 512)
    for s in range(4):
        aq_o[:, s * LANES:(s + 1) * LANES] = rope_att(head_norm(slab(z, s), qg)).astype(sd)
    az_o[...] = mm(C_AZ, 512).astype(sd)

    z = mm(C_AK, 256)
    k = rope_att(head_norm(slab(z, 0), kg))
    ak_o[...] = k
    akb_o[...] = k.astype(BF16)
    v = slab(z, 1)
    av_o[...] = v
    avb_o[...] = v.astype(BF16)

    z = mm(C_IQ, 512)
    for s in range(2):
        iq_o[:, s * LANES:(s + 1) * LANES] = rope_att(slab(z, s)).astype(sd)
    ikk = rope_att(slab(z, 2))
    ikk_o[...] = ikk.astype(BF16)
    ik_o[...] = ikk[:, :IDX_DIM]
    iw_o[...] = slab(z, 3)

    gr_o[...] = mm(C_GR, 1024).astype(sd)
    ga_o[...] = mm(C_GA, 1024).astype(sd)


def _inproj(x, gain, w_packed, tabs, qg, kg, gsum, tm, pos_period_tiles, sd):
    n, d = x.shape
    grid = (n // tm,)
    row = lambda i: (i, 0)
    const = lambda i: (0, 0)
    tab = lambda i: (i % pos_period_tiles, 0)
    in_specs = [
        pl.BlockSpec((tm, d), row),
        pl.BlockSpec((1, d), const),
        pl.BlockSpec((d, W_PACKED), const),
    ] + [pl.BlockSpec((tm, LANES), tab)] * 5 + [
        pl.BlockSpec((1, LANES), const),
        pl.BlockSpec((1, LANES), const),
        pl.BlockSpec((LANES, LANES), const),
    ]
    widths = [(512, sd)] * 6 + [(128, BF16), (128, BF16), (256, sd), (128, BF16), (1024, sd), (1024, sd),
                                (128, F32), (128, F32), (IDX_DIM, F32), (128, F32)]
    out_shape = [jax.ShapeDtypeStruct((n, w), dt) for w, dt in widths]
    out_specs = [pl.BlockSpec((tm, w), row) for w, _ in widths]
    return pl.pallas_call(
        _inproj_kernel,
        grid=grid,
        in_specs=in_specs,
        out_specs=out_specs,
        out_shape=out_shape,
        compiler_params=pltpu.CompilerParams(dimension_semantics=("arbitrary",),
                                             vmem_limit_bytes=VMEM_LIMIT),
        name="inproj",
    )(x, gain, w_packed, *tabs, qg, kg, gsum)


def _retention_kernel(q_ref, k_ref, v_ref, z_ref, s0_ref, di_ref, dq_ref, dk_ref, ds_ref,
                      o_ref, sout_ref, s_scr):
    c = pl.program_id(1)
    nc = pl.num_programs(1)

    @pl.when(c == 0)
    def _():
        s_scr[...] = s0_ref[0]

    for h in range(RET_HEADS):
        hs = slice(h * LANES, (h + 1) * LANES)
        q = q_ref[:, hs].astype(F32)
        k = k_ref[:, hs].astype(F32)
        v = v_ref[:, hs].astype(BF16)
        s_old = s_scr[h]
        inner = lax.dot_general(q.astype(BF16), k.astype(BF16), NT_DIMS,
                                preferred_element_type=F32) * di_ref[h]
        o = (jnp.dot(inner.astype(BF16), v, preferred_element_type=F32)
             + jnp.dot((q * dq_ref[:, hs]).astype(BF16), s_old.astype(BF16),
                       preferred_element_type=F32))
        kd = (k * dk_ref[:, hs]).T.astype(BF16)
        s_scr[h] = s_old * ds_ref[:, hs] + jnp.dot(kd, v, preferred_element_type=F32)
        mu = jnp.mean(o, axis=-1, keepdims=True)
        cen = o - mu
        var = jnp.mean(cen * cen, axis=-1, keepdims=True)
        gn = cen * lax.rsqrt(var + GN_EPS)
        o_ref[:, hs] = (gn * _silu(z_ref[:, hs].astype(F32))).astype(o_ref.dtype)

    @pl.when(c == nc - 1)
    def _():
        sout_ref[0] = s_scr[...]


def _retention(rq, rk, rv, rz, s0, decay, nb, nchunks, out_dtype):
    di, dq, dk, ds = decay
    n = rq.shape[0]
    c = RET_CHUNK
    blk = pl.BlockSpec((c, 512), lambda b, j: (b * nchunks + j, 0))
    st = pl.BlockSpec((1, RET_HEADS, RET_DK, RET_DV), lambda b, j: (b, 0, 0, 0))
    const2 = lambda b, j: (0, 0)
    return pl.pallas_call(
        _retention_kernel,
        grid=(nb, nchunks),
        in_specs=[blk, blk, blk, blk, st,
                  pl.BlockSpec((RET_HEADS, c, c), lambda b, j: (0, 0, 0)),
                  pl.BlockSpec((c, 512), const2),
                  pl.BlockSpec((c, 512), const2),
                  pl.BlockSpec((1, 512), const2)],
        out_specs=[blk, st],
        out_shape=[jax.ShapeDtypeStruct((n, 512), out_dtype),
                   jax.ShapeDtypeStruct((nb, RET_HEADS, RET_DK, RET_DV), F32)],
        scratch_shapes=[pltpu.VMEM((RET_HEADS, RET_DK, RET_DV), F32)],
        compiler_params=pltpu.CompilerParams(dimension_semantics=("arbitrary", "arbitrary"),
                                             vmem_limit_bytes=VMEM_LIMIT),
        name="retention",
    )(rq, rk, rv, rz, s0, di, dq, dk, ds)


def _score_keys(sc):
    bits = pltpu.bitcast(sc, I32)
    return bits ^ (lax.shift_right_arithmetic(bits, 31) & 0x7FFFFFFF)


def _split_heads(x2, rows):
    lane = lax.broadcasted_iota(I32, (rows, LANES), 1)
    lo = lane < ATT_HEAD_DIM
    return jnp.where(lo, x2, 0.0), jnp.where(lo, 0.0, x2)


def _indexer_scores(iqs, ikc, iwb, rows):
    lg = lax.dot_general(iqs, ikc, NT_DIMS, preferred_element_type=F32)
    sc = None
    for h in range(IDX_HEADS):
        t = jnp.maximum(lg[h * rows:(h + 1) * rows] * (IDX_DIM ** -0.5), 0.0) * iwb[h]
        sc = t if sc is None else sc + t
    return sc


def _count(keys_ref, nchunks, kc, rows, pred):
    def body(c, acc):
        k0 = pl.multiple_of(c * kc, kc)
        kk = keys_ref[:, pl.ds(k0, kc)]
        for j in range(kc // LANES):
            acc = acc + jnp.where(pred(kk[:, j * LANES:(j + 1) * LANES]), 1.0, 0.0)
        return acc
    acc = lax.fori_loop(0, nchunks, body, jnp.zeros((rows, LANES), F32))
    return jnp.broadcast_to(jnp.sum(acc, axis=1, keepdims=True), (rows, LANES))


def _kth_largest(keys_ref, nchunks, kc, rows, topk):
    kf = float(topk)
    c0 = _count(keys_ref, nchunks, kc, rows, lambda kk: kk >= 0)
    tau = jnp.where(c0 >= kf, 0, INT_MIN).astype(I32)

    def bitstep(b, tau):
        cand = tau | lax.shift_left(jnp.int32(1), 30 - b)
        cnt = _count(keys_ref, nchunks, kc, rows, lambda kk: kk >= cand)
        return jnp.where(cnt >= kf, cand, tau)

    tau = lax.fori_loop(0, 31, bitstep, tau)
    n_gt = _count(keys_ref, nchunks, kc, rows, lambda kk: kk > tau)
    return tau, kf - n_gt


def _select_bias(kk, tau, need, tie_carry, tri, causal, kc, rows):
    nslab = kc // LANES
    eq = [kk[:, j * LANES:(j + 1) * LANES] == tau for j in range(nslab)]
    eqf = jnp.concatenate([jnp.where(e, 1.0, 0.0) for e in eq], axis=1).astype(BF16)
    rank = jnp.dot(eqf, tri, preferred_element_type=F32)
    total = jnp.dot(eqf, jnp.ones((kc, LANES), BF16), preferred_element_type=F32)
    bias = []
    for j in range(nslab):
        sl = slice(j * LANES, (j + 1) * LANES)
        tie_ok = jnp.where((rank[:, sl] + tie_carry) <= need, 0.0, NEG_BIG)
        b = jnp.where(kk[:, sl] > tau, 0.0, jnp.where(eq[j], tie_ok, NEG_BIG))
        bias.append(jnp.where(causal[:, sl], b, NEG_BIG))
    return bias, tie_carry + total


def _attn_prompt_kernel(aq_ref, iq_ref, iw_ref, az_ref, kb_ref, vb_ref, ik_ref, tri_ref, ga_ref,
                        keys_ref, m_ref, l_ref, acc_ref, *, topk):
    tq, kc = Q_TILE, KEY_CHUNK
    nslab = kc // LANES
    i = pl.program_id(1)
    nkc = lax.div(i * tq + tq + kc - 1, kc)
    qpos = i * tq + lax.broadcasted_iota(I32, (tq, kc), 0)
    col = lax.broadcasted_iota(I32, (tq, kc), 1)

    iq = iq_ref[...].astype(F32)
    parts = []
    for s in range(2):
        parts += list(_split_heads(iq[:, s * LANES:(s + 1) * LANES], tq))
    iqs = jnp.concatenate(parts, axis=0).astype(BF16)
    iw = iw_ref[...] * (IDX_HEADS ** -0.5)
    iwb = [jnp.broadcast_to(iw[:, h:h + 1], (tq, kc)) for h in range(IDX_HEADS)]

    def scores(c, carry):
        k0 = pl.multiple_of(c * kc, kc)
        sc = _indexer_scores(iqs, ik_ref[pl.ds(k0, kc), :], iwb, tq)
        sc = jnp.where(k0 + col <= qpos, sc, -jnp.inf)
        keys_ref[:, pl.ds(k0, kc)] = _score_keys(sc)
        return carry

    lax.fori_loop(0, nkc, scores, 0)
    tau, need = _kth_largest(keys_ref, nkc, kc, tq, topk)

    aq = aq_ref[...].astype(F32) * (ATT_HEAD_DIM ** -0.5)
    lo_heads, hi_heads = [], []
    for s in range(4):
        lo, hi = _split_heads(aq[:, s * LANES:(s + 1) * LANES], tq)
        lo_heads.append(lo)
        hi_heads.append(hi)
    qs = jnp.concatenate(lo_heads + hi_heads, axis=0).astype(BF16)

    m_ref[...] = jnp.full(m_ref.shape, NEG_BIG, F32)
    l_ref[...] = jnp.zeros(l_ref.shape, F32)
    acc_ref[...] = jnp.zeros(acc_ref.shape, F32)
    tri = tri_ref[...]

    def attend(c, tie_carry):
        k0 = pl.multiple_of(c * kc, kc)
        kk = keys_ref[:, pl.ds(k0, kc)]
        bias, tie_carry = _select_bias(kk, tau, need, tie_carry, tri, k0 + col <= qpos, kc, tq)
        s = lax.dot_general(qs, kb_ref[pl.ds(k0, kc), :], NT_DIMS, preferred_element_type=F32)
        p_rows, alphas = [], []
        for h in range(ATT_HEADS):
            rs = slice(h * tq, (h + 1) * tq)
            sh = [s[rs, j * LANES:(j + 1) * LANES] + bias[j] for j in range(nslab)]
            mx = sh[0]
            for j in range(1, nslab):
                mx = jnp.maximum(mx, sh[j])
            m_old = m_ref[rs, :]
            m_new = jnp.maximum(m_old, jnp.broadcast_to(jnp.max(mx, axis=1, keepdims=True),
                                                        (tq, LANES)))
            alpha = jnp.exp(m_old - m_new)
            p = [jnp.exp(x - m_new) for x in sh]
            psum = p[0]
            for j in range(1, nslab):
                psum = psum + p[j]
            l_ref[rs, :] = alpha * l_ref[rs, :] + jnp.broadcast_to(
                jnp.sum(psum, axis=1, keepdims=True), (tq, LANES))
            m_ref[rs, :] = m_new
            p_rows.append(jnp.concatenate(p, axis=1).astype(BF16))
            alphas.append(alpha)
        pv = jnp.dot(jnp.concatenate(p_rows, axis=0), vb_ref[pl.ds(k0, kc), :],
                     preferred_element_type=F32)
        acc_ref[...] = jnp.concatenate(alphas, axis=0) * acc_ref[...] + pv
        return tie_carry

    lax.fori_loop(0, nkc, attend, jnp.zeros((tq, LANES), F32))

    lane = lax.broadcasted_iota(I32, (tq, LANES), 1)
    for s in range(4):
        lo = acc_ref[s * tq:(s + 1) * tq, :] / l_ref[s * tq:(s + 1) * tq, :]
        hi = acc_ref[(s + 4) * tq:(s + 5) * tq, :] / l_ref[(s + 4) * tq:(s + 5) * tq, :]
        o = jnp.where(lane < ATT_HEAD_DIM, lo, hi)
        az = az_ref[:, s * LANES:(s + 1) * LANES].astype(F32)
        ga_ref[:, s * LANES:(s + 1) * LANES] = (o * _silu(az)).astype(ga_ref.dtype)


def _attn_prompt(aq, iq, iw, az, akb, avb, ikk, tri, nb, t, topk):
    n = aq.shape[0]
    nq = t // Q_TILE
    qrow = lambda b, i: (b * nq + i, 0)
    seq = lambda b, i: (b, 0)
    return pl.pallas_call(
        functools.partial(_attn_prompt_kernel, topk=topk),
        grid=(nb, nq),
        in_specs=[pl.BlockSpec((Q_TILE, 512), qrow),
                  pl.BlockSpec((Q_TILE, 256), qrow),
                  pl.BlockSpec((Q_TILE, LANES), qrow),
                  pl.BlockSpec((Q_TILE, 512), qrow),
                  pl.BlockSpec((t, LANES), seq),
                  pl.BlockSpec((t, LANES), seq),
                  pl.BlockSpec((t, LANES), seq),
                  pl.BlockSpec((KEY_CHUNK, KEY_CHUNK), lambda b, i: (0, 0))],
        out_specs=pl.BlockSpec((Q_TILE, 512), qrow),
        out_shape=jax.ShapeDtypeStruct((n, 512), aq.dtype),
        scratch_shapes=[pltpu.VMEM((Q_TILE, t), I32),
                        pltpu.VMEM((ATT_HEADS * Q_TILE, LANES), F32),
                        pltpu.VMEM((ATT_HEADS * Q_TILE, LANES), F32),
                        pltpu.VMEM((ATT_HEADS * Q_TILE, LANES), F32)],
        compiler_params=pltpu.CompilerParams(dimension_semantics=("arbitrary", "arbitrary"),
                                             vmem_limit_bytes=VMEM_LIMIT),
        name="attn_prompt",
    )(aq, iq, iw, az, akb, avb, ikk, tri)


def _sample_scores_kernel(pt_ref, iq_ref, iw_ref, iknew_ref, *rest, n_pages, page, ts):
    del pt_ref
    page_refs, keys_ref = rest[:n_pages], rest[n_pages]
    past = n_pages * page
    iq = iq_ref[...].astype(F32)
    heads = [iq[:, h * IDX_DIM:(h + 1) * IDX_DIM] for h in range(IDX_HEADS)]
    iqs = jnp.concatenate(heads, axis=0).astype(BF16)
    iw = iw_ref[...] * (IDX_HEADS ** -0.5)
    iwb = [jnp.broadcast_to(iw[:, h:h + 1], (ts, page)) for h in range(IDX_HEADS)]
    for p in range(n_pages):
        sc = _indexer_scores(iqs, page_refs[p][...].astype(BF16), iwb, ts)
        keys_ref[:, p * page:(p + 1) * page] = _score_keys(sc)
    new = jnp.concatenate([iknew_ref[...], jnp.zeros((page - ts, IDX_DIM), F32)], axis=0)
    sc = _indexer_scores(iqs, new.astype(BF16), iwb, ts)
    row = lax.broadcasted_iota(I32, (ts, page), 0)
    col = lax.broadcasted_iota(I32, (ts, page), 1)
    sc = jnp.where(col <= row, sc, -jnp.inf)
    keys_ref[:, past:past + page] = _score_keys(sc)


def _sample_scores(page_table, iq, iw, ik_new, cache_idx, layer, ts):
    nb, n_pages = page_table.shape
    page = cache_idx.shape[2]
    lp = (n_pages + 1) * page
    row = lambda b, pt: (b, 0)
    page_specs = [pl.BlockSpec((None, None, page, IDX_DIM),
                               functools.partial(lambda b, pt, p: (layer, pt[b, p], 0, 0), p=p))
                  for p in range(n_pages)]
    grid_spec = pltpu.PrefetchScalarGridSpec(
        num_scalar_prefetch=1,
        grid=(nb,),
        in_specs=[pl.BlockSpec((ts, 256), row),
                  pl.BlockSpec((ts, LANES), row),
                  pl.BlockSpec((ts, IDX_DIM), row)] + page_specs,
        out_specs=pl.BlockSpec((ts, lp), row),
    )
    return pl.pallas_call(
        functools.partial(_sample_scores_kernel, n_pages=n_pages, page=page, ts=ts),
        grid_spec=grid_spec,
        out_shape=jax.ShapeDtypeStruct((nb * ts, lp), I32),
        compiler_params=pltpu.CompilerParams(dimension_semantics=("arbitrary",),
                                             vmem_limit_bytes=VMEM_LIMIT),
        name="sample_scores",
    )(page_table, iq, iw, ik_new, *([cache_idx] * n_pages))


def _sample_select_kernel(keys_ref, tri_ref, bias_ref, *, topk, ts, past, lp):
    rows = keys_ref.shape[0]
    nch = lp // LANES
    tau, need = _kth_largest(keys_ref, nch, LANES, rows, topk)
    qpos = past + lax.rem(lax.broadcasted_iota(I32, (rows, LANES), 0), ts)
    col = lax.broadcasted_iota(I32, (rows, LANES), 1)
    tri = tri_ref[...]

    def body(c, tie_carry):
        k0 = pl.multiple_of(c * LANES, LANES)
        kk = keys_ref[:, pl.ds(k0, LANES)]
        bias, tie_carry = _select_bias(kk, tau, need, tie_carry, tri, k0 + col <= qpos, LANES, rows)
        bias_ref[:, pl.ds(k0, LANES)] = bias[0]
        return tie_carry

    lax.fori_loop(0, nch, body, jnp.zeros((rows, LANES), F32))


def _sample_select(keys, tri, topk, ts, past):
    n, lp = keys.shape
    rows = 128
    return pl.pallas_call(
        functools.partial(_sample_select_kernel, topk=topk, ts=ts, past=past, lp=lp),
        grid=(n // rows,),
        in_specs=[pl.BlockSpec((rows, lp), lambda i: (i, 0)),
                  pl.BlockSpec((LANES, LANES), lambda i: (0, 0))],
        out_specs=pl.BlockSpec((rows, lp), lambda i: (i, 0)),
        out_shape=jax.ShapeDtypeStruct((n, lp), F32),
        compiler_params=pltpu.CompilerParams(dimension_semantics=("arbitrary",),
                                             vmem_limit_bytes=VMEM_LIMIT),
        name="sample_select",
    )(keys, tri)


def _sample_attn_kernel(pt_ref, aq_ref, az_ref, bias_ref, knew_ref, vnew_ref, *rest,
                        n_pages, page, ts):
    del pt_ref
    k_refs, v_refs = rest[:n_pages], rest[n_pages:2 * n_pages]
    ga_ref, s_scr = rest[2 * n_pages], rest[2 * n_pages + 1]
    rows = ATT_HEADS * ts
    aq = aq_ref[...].astype(F32) * (ATT_HEAD_DIM ** -0.5)
    lo_heads, hi_heads = [], []
    for s in range(4):
        lo, hi = _split_heads(aq[:, s * LANES:(s + 1) * LANES], ts)
        lo_heads.append(lo)
        hi_heads.append(hi)
    qs = jnp.concatenate(lo_heads + hi_heads, axis=0).astype(BF16)
    pad = jnp.zeros((page - ts, LANES), F32)
    k_new = jnp.concatenate([knew_ref[...], pad], axis=0).astype(BF16)
    v_new = jnp.concatenate([vnew_ref[...], pad], axis=0).astype(BF16)

    def chunk_k(c):
        return k_refs[c][...].astype(BF16) if c < n_pages else k_new

    def chunk_v(c):
        return v_refs[c][...].astype(BF16) if c < n_pages else v_new

    mx = jnp.full((rows, LANES), NEG_BIG, F32)
    for c in range(n_pages + 1):
        sl = slice(c * page, (c + 1) * page)
        s = lax.dot_general(qs, chunk_k(c), NT_DIMS, preferred_element_type=F32)
        s = s + jnp.concatenate([bias_ref[:, sl]] * ATT_HEADS, axis=0)
        s_scr[:, sl] = s
        mx = jnp.maximum(mx, s)
    m = jnp.broadcast_to(jnp.max(mx, axis=1, keepdims=True), (rows, LANES))
    acc = jnp.zeros((rows, LANES), F32)
    psum = jnp.zeros((rows, LANES), F32)
    for c in range(n_pages + 1):
        p = jnp.exp(s_scr[:, c * page:(c + 1) * page] - m)
        psum = psum + p
        acc = acc + jnp.dot(p.astype(BF16), chunk_v(c), preferred_element_type=F32)
    o = acc / jnp.broadcast_to(jnp.sum(psum, axis=1, keepdims=True), (rows, LANES))
    lane = lax.broadcasted_iota(I32, (ts, LANES), 1)
    for s in range(4):
        oo = jnp.where(lane < ATT_HEAD_DIM, o[s * ts:(s + 1) * ts], o[(s + 4) * ts:(s + 5) * ts])
        az = az_ref[:, s * LANES:(s + 1) * LANES].astype(F32)
        ga_ref[:, s * LANES:(s + 1) * LANES] = (oo * _silu(az)).astype(ga_ref.dtype)


def _sample_attn(page_table, aq, az, bias, k_new, v_new, cache_k, cache_v, layer, ts):
    nb, n_pages = page_table.shape
    page = cache_k.shape[2]
    lp = (n_pages + 1) * page
    row = lambda b, pt: (b, 0)
    page_specs = [pl.BlockSpec((None, None, page, LANES),
                               functools.partial(lambda b, pt, p: (layer, pt[b, p], 0, 0), p=p))
                  for p in range(n_pages)]
    grid_spec = pltpu.PrefetchScalarGridSpec(
        num_scalar_prefetch=1,
        grid=(nb,),
        in_specs=[pl.BlockSpec((ts, 512), row),
                  pl.BlockSpec((ts, 512), row),
                  pl.BlockSpec((ts, lp), row),
                  pl.BlockSpec((ts, LANES), row),
                  pl.BlockSpec((ts, LANES), row)] + page_specs + page_specs,
        out_specs=pl.BlockSpec((ts, 512), row),
        scratch_shapes=[pltpu.VMEM((ATT_HEADS * ts, lp), F32)],
    )
    return pl.pallas_call(
        functools.partial(_sample_attn_kernel, n_pages=n_pages, page=page, ts=ts),
        grid_spec=grid_spec,
        out_shape=jax.ShapeDtypeStruct((nb * ts, 512), aq.dtype),
        compiler_params=pltpu.CompilerParams(dimension_semantics=("arbitrary",),
                                             vmem_limit_bytes=VMEM_LIMIT),
        name="sample_attn",
    )(page_table, aq, az, bias, k_new, v_new, *([cache_k] * n_pages), *([cache_v] * n_pages))


def _merge_kernel(x_ref, p_ref, gr_ref, ga_ref, br_ref, ba_ref, wor_ref, woa_ref, wout_ref,
                  wpg_ref, wpp_ref, y_ref):
    u_r = jnp.dot(br_ref[...].astype(BF16), wor_ref[...], preferred_element_type=F32)
    u_a = jnp.dot(ba_ref[...].astype(BF16), woa_ref[...], preferred_element_type=F32)
    m = (jax.nn.sigmoid(gr_ref[...].astype(F32)) * u_r
         + jax.nn.sigmoid(ga_ref[...].astype(F32)) * u_a)
    x1 = x_ref[...] + jnp.dot(m.astype(BF16), wout_ref[...], preferred_element_type=F32)
    gate = jax.nn.sigmoid(jnp.dot(x1.astype(BF16), wpg_ref[...], preferred_element_type=F32))
    y_ref[...] = x1 + gate * jnp.dot(p_ref[...].astype(BF16), wpp_ref[...],
                                     preferred_element_type=F32)


def _merge(x, p, gr, ga, br, ba, wor, woa, wout, wpg, wpp, tm):
    n, d = x.shape
    row = lambda i: (i, 0)
    const = lambda i: (0, 0)
    return pl.pallas_call(
        _merge_kernel,
        grid=(n // tm,),
        in_specs=[pl.BlockSpec((tm, d), row),
                  pl.BlockSpec((tm, p.shape[1]), row),
                  pl.BlockSpec((tm, d), row),
                  pl.BlockSpec((tm, d), row),
                  pl.BlockSpec((tm, 512), row),
                  pl.BlockSpec((tm, 512), row),
                  pl.BlockSpec(wor.shape, const),
                  pl.BlockSpec(woa.shape, const),
                  pl.BlockSpec(wout.shape, const),
                  pl.BlockSpec(wpg.shape, const),
                  pl.BlockSpec(wpp.shape, const)],
        out_specs=pl.BlockSpec((tm, d), row),
        out_shape=jax.ShapeDtypeStruct((n, d), F32),
        compiler_params=pltpu.CompilerParams(dimension_semantics=("arbitrary",),
                                             vmem_limit_bytes=VMEM_LIMIT),
        name="merge",
    )(x, p, gr, ga, br, ba, wor, woa, wout, wpg, wpp)


def _pair_heads_cols(m):
    d = m.shape[0]
    m = m.reshape(d, ATT_KV_HEADS, ATT_HEADS // ATT_KV_HEADS, ATT_HEAD_DIM)
    return jnp.concatenate([m[:, 0], m[:, 1]], axis=2).reshape(d, ATT_HEADS * ATT_HEAD_DIM)


def _pack_w_in(w):
    sizes = (512, 512, 512, 512, 512, 128, 128, 512, 256, 64, 4, 1024, 1024)
    offs, o = [], 0
    for s in sizes:
        offs.append((o, o + s))
        o += s
    rq, rk, rv, rz, aq, ak, av, az, iq, ik, iw, gr, ga = [w[:, a:b] for a, b in offs]
    iwp = jnp.pad(iw, ((0, 0), (0, LANES - IDX_HEADS)))
    packed = jnp.concatenate([rq, rk, rv, rz, _pair_heads_cols(aq), _pair_heads_cols(az), ak, av,
                              iq, ik, ik, iwp, gr, ga], axis=1)
    return packed.astype(BF16)


def _rope_tables(pos):
    pf = pos.astype(F32)[:, None]
    half = RET_DK // 2
    freqs = jnp.exp(-math.log(RET_THETA) * jnp.arange(half, dtype=F32) / half)
    ang = pf * freqs[None, :]
    cos, sin = jnp.cos(ang), jnp.sin(ang)
    rc = jnp.concatenate([cos, cos], axis=1)
    rs = jnp.concatenate([-sin, sin], axis=1)
    half = ROPE_DIM // 2
    freqs = jnp.exp(-math.log(ROPE_THETA) * jnp.arange(half, dtype=F32) / half)
    ang = pf * freqs[None, :]
    cos, sin = jnp.cos(ang), jnp.sin(ang)
    r = pos.shape[0]
    rest = ATT_HEAD_DIM - ROPE_DIM
    one, zero, zh = jnp.ones((r, rest), F32), jnp.zeros((r, rest), F32), jnp.zeros((r, half), F32)
    ac = jnp.concatenate([cos, cos, one], axis=1)
    as1 = jnp.concatenate([-sin, zh, zero], axis=1)
    as2 = jnp.concatenate([zh, sin, zero], axis=1)
    tile2 = lambda a: jnp.concatenate([a, a], axis=1)
    return rc, rs, tile2(ac), tile2(as1), tile2(as2)


def _decay_tables(c_eff):
    h = RET_HEADS
    log_g = jnp.log1p(-jnp.exp2(-5.0 - jnp.arange(h, dtype=F32)))
    c = jnp.arange(RET_CHUNK, dtype=F32)
    diff = c[:, None] - c[None, :]
    di = jnp.where(diff[None] >= 0, jnp.exp(jnp.maximum(diff, 0.0)[None] * log_g[:, None, None]), 0.0)
    dq = jnp.exp((c[:, None] + 1.0) * log_g[None, :])
    dk = jnp.exp((c_eff - 1.0 - c)[:, None] * log_g[None, :])
    dk = jnp.where(c[:, None] < c_eff, dk, 0.0)
    ds = jnp.exp(c_eff * log_g)
    rep = lambda a: jnp.repeat(a, RET_DK, axis=-1)
    return di, rep(dq), rep(dk), rep(ds[None, :])


def _tri(n):
    r = lax.broadcasted_iota(I32, (n, n), 0)
    c = lax.broadcasted_iota(I32, (n, n), 1)
    return jnp.where(r <= c, 1.0, 0.0).astype(BF16)


def _group_sum_matrix():
    r = lax.broadcasted_iota(I32, (LANES, LANES), 0) // ATT_HEAD_DIM
    c = lax.broadcasted_iota(I32, (LANES, LANES), 1) // ATT_HEAD_DIM
    return jnp.where(r == c, 1.0, 0.0).astype(BF16)


def _layer_weights(i, norm_gain, w_in, q_norm_gain, k_norm_gain, w_o_ret, w_o_att, w_out,
                   w_ple_gate, w_ple_proj):
    woa = w_o_att[i].reshape(ATT_KV_HEADS, ATT_HEADS // ATT_KV_HEADS, ATT_HEAD_DIM, -1)
    woa = jnp.concatenate([woa[0], woa[1]], axis=1).reshape(ATT_HEADS * ATT_HEAD_DIM, -1)
    return dict(
        gain=norm_gain[i][None, :],
        w_in=_pack_w_in(w_in[i]),
        qg=jnp.tile(q_norm_gain[i], 2)[None, :],
        kg=jnp.tile(k_norm_gain[i], 2)[None, :],
        wor=w_o_ret[i].astype(BF16),
        woa=woa.astype(BF16),
        wout=w_out[i].astype(BF16),
        wpg=w_ple_gate[i].astype(BF16),
        wpp=w_ple_proj[i].astype(BF16),
    )


def kernel(x_prompt, x_sample, cache_k, cache_v, cache_idx_k, state_ret, page_table, p_prompt,
           p_sample, norm_gain, w_in, q_norm_gain, k_norm_gain, w_o_ret, w_o_att, w_out, w_ple_gate,
           w_ple_proj):
    bp, tp, d = x_prompt.shape
    bs, ts, _ = x_sample.shape
    depth = w_in.shape[0]
    n_pool, page = cache_k.shape[1], cache_k.shape[2]
    n_pages = page_table.shape[1]
    past = n_pages * page
    topk_p = min(TOPK_MAX, tp // 4)
    topk_s = min(TOPK_MAX, (past + ts) // 4)
    assert tp % KEY_CHUNK == 0 and tp % RET_CHUNK == 0 and ts <= page and ts % 8 == 0

    np_, ns_ = bp * tp, bs * ts
    tm_p = 512 if np_ % 512 == 0 else Q_TILE
    tm_s = 256 if ns_ % 256 == 0 else ns_
    assert tm_s % ts == 0 and tp % tm_p == 0

    gsum = _group_sum_matrix()
    tabs_p = _rope_tables(jnp.arange(tp, dtype=I32))
    tabs_s = _rope_tables(past + (jnp.arange(tm_s, dtype=I32) % ts))
    decay_p = _decay_tables(float(RET_CHUNK))
    decay_s = _decay_tables(float(ts))
    tri_p, tri_s = _tri(KEY_CHUNK), _tri(LANES)
    ck = cache_k.reshape(depth, n_pool, page, LANES)
    cv = cache_v.reshape(depth, n_pool, page, LANES)
    s0_p = jnp.zeros((bp, RET_HEADS, RET_DK, RET_DV), F32)

    xp = x_prompt.reshape(np_, d)
    xs = x_sample.reshape(ns_, d)
    outs = {k: [] for k in ("kp", "vp", "ikp", "sp", "ks", "vs", "iks", "ss")}

    def pad_rows(a):
        w = a.shape[1]
        return jnp.pad(a.reshape(bs, ts, w), ((0, 0), (0, RET_CHUNK - ts), (0, 0))).reshape(
            bs * RET_CHUNK, w)

    for i in range(depth):
        lw = _layer_weights(i, norm_gain, w_in, q_norm_gain, k_norm_gain, w_o_ret, w_o_att, w_out,
                            w_ple_gate, w_ple_proj)

        (rq, rk, rv, rz, aq, az, akb, avb, iq, ikk, gr, ga, ak, av, ik, iw) = _inproj(
            xp, lw["gain"], lw["w_in"], tabs_p, lw["qg"], lw["kg"], gsum, tm_p, tp // tm_p, BF16)
        b_r, s_new = _retention(rq, rk, rv, rz, s0_p, decay_p, bp, tp // RET_CHUNK, BF16)
        b_a = _attn_prompt(aq, iq, iw, az, akb, avb, ikk, tri_p, bp, tp, topk_p)
        xp = _merge(xp, p_prompt[i].reshape(np_, -1), gr, ga, b_r, b_a, lw["wor"], lw["woa"],
                    lw["wout"], lw["wpg"], lw["wpp"], tm_p)
        outs["kp"].append(ak.reshape(bp, tp, ATT_KV_HEADS, ATT_HEAD_DIM))
        outs["vp"].append(av.reshape(bp, tp, ATT_KV_HEADS, ATT_HEAD_DIM))
        outs["ikp"].append(ik.reshape(bp, tp, IDX_DIM))
        outs["sp"].append(s_new)

        (rq, rk, rv, rz, aq, az, akb, avb, iq, ikk, gr, ga, ak, av, ik, iw) = _inproj(
            xs, lw["gain"], lw["w_in"], tabs_s, lw["qg"], lw["kg"], gsum, tm_s, 1, F32)
        b_r, s_new = _retention(pad_rows(rq), pad_rows(rk), pad_rows(rv), pad_rows(rz),
                                state_ret[i], decay_s, bs, 1, F32)
        b_r = b_r.reshape(bs, RET_CHUNK, 512)[:, :ts].reshape(ns_, 512)
        keys = _sample_scores(page_table, iq, iw, ik, cache_idx_k, i, ts)
        bias = _sample_select(keys, tri_s, topk_s, ts, past)
        b_a = _sample_attn(page_table, aq, az, bias, ak, av, ck, cv, i, ts)
        xs = _merge(xs, p_sample[i].reshape(ns_, -1), gr, ga, b_r, b_a, lw["wor"], lw["woa"],
                    lw["wout"], lw["wpg"], lw["wpp"], tm_s)
        outs["ks"].append(ak.reshape(bs, ts, ATT_KV_HEADS, ATT_HEAD_DIM))
        outs["vs"].append(av.reshape(bs, ts, ATT_KV_HEADS, ATT_HEAD_DIM))
        outs["iks"].append(ik.reshape(bs, ts, IDX_DIM))
        outs["ss"].append(s_new)

    st = lambda k: jnp.stack(outs[k])
    return (xp.reshape(bp, tp, d), xs.reshape(bs, ts, d), st("kp"), st("vp"), st("ikp"), st("sp"),
            st("ks"), st("vs"), st("iks"), st("ss"))
```

```python
import functools
import math

import jax
import jax.numpy as jnp
from jax import lax
from jax.experimental import pallas as pl
from jax.experimental.pallas import tpu as pltpu

F32 = jnp.float32
BF16 = jnp.bfloat16
I32 = jnp.int32

RET_HEADS = 4
RET_DK = 128
RET_DV = 128
RET_CHUNK = 128
RET_THETA = 10000.0
ATT_HEADS = 8
ATT_KV_HEADS = 2
ATT_HEAD_DIM = 64
ROPE_THETA = 500000.0
ROPE_DIM = ATT_HEAD_DIM // 4
IDX_HEADS = 4
IDX_DIM = 64
TOPK_MAX = 256
NORM_EPS = 1e-6
GN_EPS = 1e-5

LANES = 128
Q_TILE = 128
KEY_CHUNK = 512
VMEM_LIMIT = 56 * 1024 * 1024
NEG_BIG = -1e30
INT_MIN = -(2 ** 31)

C_RQ, C_RK, C_RV, C_RZ = 0, 512, 1024, 1536
C_AQ, C_AZ, C_AK, C_AV = 2048, 2560, 3072, 3200
C_IQ, C_IKK, C_IW, C_GR, C_GA = 3328, 3584, 3712, 3840, 4864
W_PACKED = 5888

NT_DIMS = (((1,), (1,)), ((), ()))


def _silu(x):
    return x * jax.nn.sigmoid(x)


def _inproj_kernel(x_ref, g_ref, w_ref, rc_ref, rs_ref, ac_ref, as1_ref, as2_ref, qg_ref, kg_ref,
                   gsum_ref,
                   rq_o, rk_o, rv_o, rz_o, aq_o, az_o, akb_o, avb_o, iq_o, ikk_o, gr_o, ga_o,
                   ak_o, av_o, ik_o, iw_o):
    sd = rq_o.dtype
    x = x_ref[...]
    ms = jnp.mean(x * x, axis=-1, keepdims=True)
    hb = ((x * lax.rsqrt(ms + NORM_EPS)) * g_ref[...]).astype(BF16)

    def mm(c0, width):
        return lax.dot_general(hb, w_ref[c0:c0 + width, :], NT_DIMS, preferred_element_type=F32)

    rc, rs = rc_ref[...], rs_ref[...]
    ac, as1, as2 = ac_ref[...], as1_ref[...], as2_ref[...]
    gsum = gsum_ref[...]

    def rope_ret(z):
        return z * rc + pltpu.roll(z, 64, 1) * rs

    def rope_att(z):
        return z * ac + pltpu.roll(z, LANES - 8, 1) * as1 + pltpu.roll(z, 8, 1) * as2

    def head_norm(z, gain):
        sq = z * z
        hi = sq.astype(BF16)
        lo = (sq - hi.astype(F32)).astype(BF16)
        ssq = (jnp.dot(hi, gsum, preferred_element_type=F32)
               + jnp.dot(lo, gsum, preferred_element_type=F32))
        return (z * lax.rsqrt(ssq * (1.0 / ATT_HEAD_DIM) + NORM_EPS)) * gain

    def slab(z, s):
        return z[:, s * LANES:(s + 1) * LANES]

    z = mm(C_RQ, 512)
    for s in range(4):
        rq_o[:, s * LANES:(s + 1) * LANES] = rope_ret(slab(z, s)).astype(sd)
    z = mm(C_RK, 512)
    for s in range(4):
        rk_o[:, s * LANES:(s + 1) * LANES] = (rope_ret(slab(z, s)) * (RET_DK ** -0.5)).astype(sd)
    rv_o[...] = mm(C_RV, 512).astype(sd)
    rz_o[...] = mm(C_RZ, 512).astype(sd)

    qg, kg = qg_ref[...], kg_ref[...]
    z = mm(C_AQ, 512)
    for s in range(4):
        aq_o[:, s * LANES:(s + 1) * LANES] = rope_att(head_norm(slab(z, s), qg)).astype(sd)
    az_o[...] = mm(C_AZ, 512).astype(sd)

    z = mm(C_AK, 256)
    k = rope_att(head_norm(slab(z, 0), kg))
    ak_o[...] = k
    akb_o[...] = k.astype(BF16)
    v = slab(z, 1)
    av_o[...] = v
    avb_o[...] = v.astype(BF16)

    z = mm(C_IQ, 512)
    for s in range(2):
        iq_o[:, s * LANES:(s + 1) * LANES] = rope_att(slab(z, s)).astype(sd)
    ikk = rope_att(slab(z, 2))
    ikk_o[...] = ikk.astype(BF16)
    ik_o[...] = ikk[:, :IDX_DIM]
    iw_o[...] = slab(z, 3)

    gr_o[...] = mm(C_GR, 1024).astype(sd)
    ga_o[...] = mm(C_GA, 1024).astype(sd)


def _inproj(x, gain, w_packed, tabs, qg, kg, gsum, tm, pos_period_tiles, sd):
    n, d = x.shape
    grid = (n // tm,)
    row = lambda i: (i, 0)
    const = lambda i: (0, 0)
    tab = lambda i: (i % pos_period_tiles, 0)
    in_specs = [
        pl.BlockSpec((tm, d), row),
        pl.BlockSpec((1, d), const),
        pl.BlockSpec((W_PACKED, d), const),
    ] + [pl.BlockSpec((tm, LANES), tab)] * 5 + [
        pl.BlockSpec((1, LANES), const),
        pl.BlockSpec((1, LANES), const),
        pl.BlockSpec((LANES, LANES), const),
    ]
    widths = [(512, sd)] * 6 + [(128, BF16), (128, BF16), (256, sd), (128, BF16), (1024, sd), (1024, sd),
                                (128, F32), (128, F32), (IDX_DIM, F32), (128, F32)]
    out_shape = [jax.ShapeDtypeStruct((n, w), dt) for w, dt in widths]
    out_specs = [pl.BlockSpec((tm, w), row) for w, _ in widths]
    return pl.pallas_call(
        _inproj_kernel,
        grid=grid,
        in_specs=in_specs,
        out_specs=out_specs,
        out_shape=out_shape,
        compiler_params=pltpu.CompilerParams(dimension_semantics=("arbitrary",),
                                             vmem_limit_bytes=VMEM_LIMIT),
        name="inproj",
    )(x, gain, w_packed, *tabs, qg, kg, gsum)


def _retention_kernel(q_ref, k_ref, v_ref, z_ref, s0_ref, di_ref, dq_ref, dk_ref, ds_ref,
                      o_ref, sout_ref, s_scr):
    c = pl.program_id(1)
    nc = pl.num_programs(1)

    @pl.when(c == 0)
    def _():
        s_scr[...] = s0_ref[0]

    for h in range(RET_HEADS):
        hs = slice(h * LANES, (h + 1) * LANES)
        q = q_ref[:, hs].astype(F32)
        k = k_ref[:, hs].astype(F32)
        v = v_ref[:, hs].astype(BF16)
        s_old = s_scr[h]
        inner = lax.dot_general(q.astype(BF16), k.astype(BF16), NT_DIMS,
                                preferred_element_type=F32) * di_ref[h]
        o = (jnp.dot(inner.astype(BF16), v, preferred_element_type=F32)
             + jnp.dot((q * dq_ref[:, hs]).astype(BF16), s_old.astype(BF16),
                       preferred_element_type=F32))
        kd = (k * dk_ref[:, hs]).T.astype(BF16)
        s_scr[h] = s_old * ds_ref[:, hs] + jnp.dot(kd, v, preferred_element_type=F32)
        mu = jnp.mean(o, axis=-1, keepdims=True)
        cen = o - mu
        var = jnp.mean(cen * cen, axis=-1, keepdims=True)
        gn = cen * lax.rsqrt(var + GN_EPS)
        o_ref[:, hs] = (gn * _silu(z_ref[:, hs].astype(F32))).astype(o_ref.dtype)

    @pl.when(c == nc - 1)
    def _():
        sout_ref[0] = s_scr[...]


def _retention(rq, rk, rv, rz, s0, decay, nb, nchunks, out_dtype):
    di, dq, dk, ds = decay
    n = rq.shape[0]
    c = RET_CHUNK
    blk = pl.BlockSpec((c, 512), lambda b, j: (b * nchunks + j, 0))
    st = pl.BlockSpec((1, RET_HEADS, RET_DK, RET_DV), lambda b, j: (b, 0, 0, 0))
    const2 = lambda b, j: (0, 0)
    return pl.pallas_call(
        _retention_kernel,
        grid=(nb, nchunks),
        in_specs=[blk, blk, blk, blk, st,
                  pl.BlockSpec((RET_HEADS, c, c), lambda b, j: (0, 0, 0)),
                  pl.BlockSpec((c, 512), const2),
                  pl.BlockSpec((c, 512), const2),
                  pl.BlockSpec((1, 512), const2)],
        out_specs=[blk, st],
        out_shape=[jax.ShapeDtypeStruct((n, 512), out_dtype),
                   jax.ShapeDtypeStruct((nb, RET_HEADS, RET_DK, RET_DV), F32)],
        scratch_shapes=[pltpu.VMEM((RET_HEADS, RET_DK, RET_DV), F32)],
        compiler_params=pltpu.CompilerParams(dimension_semantics=("arbitrary", "arbitrary"),
                                             vmem_limit_bytes=VMEM_LIMIT),
        name="retention",
    )(rq, rk, rv, rz, s0, di, dq, dk, ds)


def _score_keys(sc):
    bits = pltpu.bitcast(sc, I32)
    return bits ^ (lax.shift_right_arithmetic(bits, 31) & 0x7FFFFFFF)


def _split_heads(x2, rows):
    lane = lax.broadcasted_iota(I32, (rows, LANES), 1)
    lo = lane < ATT_HEAD_DIM
    return jnp.where(lo, x2, 0.0), jnp.where(lo, 0.0, x2)


def _indexer_scores(iqs, ikc, iwb, rows, feature_major=False):
    if feature_major:
        lg = jnp.dot(iqs, ikc, preferred_element_type=F32)
    else:
        lg = lax.dot_general(iqs, ikc, NT_DIMS, preferred_element_type=F32)
    sc = None
    for h in range(IDX_HEADS):
        t = jnp.maximum(lg[h * rows:(h + 1) * rows] * (IDX_DIM ** -0.5), 0.0) * iwb[h]
        sc = t if sc is None else sc + t
    return sc


def _count(keys_ref, nchunks, kc, rows, pred):
    def body(c, acc):
        k0 = pl.multiple_of(c * kc, kc)
        kk = keys_ref[:, pl.ds(k0, kc)]
        for j in range(kc // LANES):
            acc = acc + jnp.where(pred(kk[:, j * LANES:(j + 1) * LANES]), 1.0, 0.0)
        return acc
    acc = lax.fori_loop(0, nchunks, body, jnp.zeros((rows, LANES), F32))
    return jnp.broadcast_to(jnp.sum(acc, axis=1, keepdims=True), (rows, LANES))


def _kth_largest(keys_ref, nchunks, kc, rows, topk):
    kf = float(topk)
    c0 = _count(keys_ref, nchunks, kc, rows, lambda kk: kk >= 0)
    tau = jnp.where(c0 >= kf, 0, INT_MIN).astype(I32)

    def bitstep(b, tau):
        cand = tau | lax.shift_left(jnp.int32(1), 30 - b)
        cnt = _count(keys_ref, nchunks, kc, rows, lambda kk: kk >= cand)
        return jnp.where(cnt >= kf, cand, tau)

    tau = lax.fori_loop(0, 31, bitstep, tau)
    n_gt = _count(keys_ref, nchunks, kc, rows, lambda kk: kk > tau)
    return tau, kf - n_gt


def _select_bias(kk, tau, need, tie_carry, tri, causal, kc, rows):
    nslab = kc // LANES
    eq = [kk[:, j * LANES:(j + 1) * LANES] == tau for j in range(nslab)]
    eqf = jnp.concatenate([jnp.where(e, 1.0, 0.0) for e in eq], axis=1).astype(BF16)
    rank = jnp.dot(eqf, tri, preferred_element_type=F32)
    total = jnp.dot(eqf, jnp.ones((kc, LANES), BF16), preferred_element_type=F32)
    bias = []
    for j in range(nslab):
        sl = slice(j * LANES, (j + 1) * LANES)
        tie_ok = jnp.where((rank[:, sl] + tie_carry) <= need, 0.0, NEG_BIG)
        b = jnp.where(kk[:, sl] > tau, 0.0, jnp.where(eq[j], tie_ok, NEG_BIG))
        bias.append(jnp.where(causal[:, sl], b, NEG_BIG))
    return bias, tie_carry + total


def _attn_prompt_kernel(aq_ref, iq_ref, iw_ref, az_ref, kb_ref, vb_ref, ik_ref, tri_ref, ga_ref,
                        keys_ref, m_ref, l_ref, acc_ref, *, topk):
    tq, kc = Q_TILE, KEY_CHUNK
    nslab = kc // LANES
    i = pl.program_id(1)
    nkc = lax.div(i * tq + tq + kc - 1, kc)
    qpos = i * tq + lax.broadcasted_iota(I32, (tq, kc), 0)
    col = lax.broadcasted_iota(I32, (tq, kc), 1)

    iq = iq_ref[...].astype(F32)
    parts = []
    for s in range(2):
        parts += list(_split_heads(iq[:, s * LANES:(s + 1) * LANES], tq))
    iqs = jnp.concatenate(parts, axis=0).astype(BF16)
    iw = iw_ref[...] * (IDX_HEADS ** -0.5)
    iwb = [jnp.broadcast_to(iw[:, h:h + 1], (tq, kc)) for h in range(IDX_HEADS)]

    def scores(c, carry):
        k0 = pl.multiple_of(c * kc, kc)
        sc = _indexer_scores(iqs, ik_ref[pl.ds(k0, kc), :], iwb, tq)
        sc = jnp.where(k0 + col <= qpos, sc, -jnp.inf)
        keys_ref[:, pl.ds(k0, kc)] = _score_keys(sc)
        return carry

    lax.fori_loop(0, nkc, scores, 0)
    tau, need = _kth_largest(keys_ref, nkc, kc, tq, topk)

    aq = aq_ref[...].astype(F32) * (ATT_HEAD_DIM ** -0.5)
    lo_heads, hi_heads = [], []
    for s in range(4):
        lo, hi = _split_heads(aq[:, s * LANES:(s + 1) * LANES], tq)
        lo_heads.append(lo)
        hi_heads.append(hi)
    qs = jnp.concatenate(lo_heads + hi_heads, axis=0).astype(BF16)

    m_ref[...] = jnp.full(m_ref.shape, NEG_BIG, F32)
    l_ref[...] = jnp.zeros(l_ref.shape, F32)
    acc_ref[...] = jnp.zeros(acc_ref.shape, F32)
    tri = tri_ref[...]

    def attend(c, tie_carry):
        k0 = pl.multiple_of(c * kc, kc)
        kk = keys_ref[:, pl.ds(k0, kc)]
        bias, tie_carry = _select_bias(kk, tau, need, tie_carry, tri, k0 + col <= qpos, kc, tq)
        s = lax.dot_general(qs, kb_ref[pl.ds(k0, kc), :], NT_DIMS, preferred_element_type=F32)
        p_rows, alphas = [], []
        for h in range(ATT_HEADS):
            rs = slice(h * tq, (h + 1) * tq)
            sh = [s[rs, j * LANES:(j + 1) * LANES] + bias[j] for j in range(nslab)]
            mx = sh[0]
            for j in range(1, nslab):
                mx = jnp.maximum(mx, sh[j])
            m_old = m_ref[rs, :]
            m_new = jnp.maximum(m_old, jnp.broadcast_to(jnp.max(mx, axis=1, keepdims=True),
                                                        (tq, LANES)))
            alpha = jnp.exp(m_old - m_new)
            p = [jnp.exp(x - m_new) for x in sh]
            psum = p[0]
            for j in range(1, nslab):
                psum = psum + p[j]
            l_ref[rs, :] = alpha * l_ref[rs, :] + jnp.broadcast_to(
                jnp.sum(psum, axis=1, keepdims=True), (tq, LANES))
            m_ref[rs, :] = m_new
            p_rows.append(jnp.concatenate(p, axis=1).astype(BF16))
            alphas.append(alpha)
        pv = jnp.dot(jnp.concatenate(p_rows, axis=0), vb_ref[pl.ds(k0, kc), :],
                     preferred_element_type=F32)
        acc_ref[...] = jnp.concatenate(alphas, axis=0) * acc_ref[...] + pv
        return tie_carry

    lax.fori_loop(0, nkc, attend, jnp.zeros((tq, LANES), F32))

    lane = lax.broadcasted_iota(I32, (tq, LANES), 1)
    for s in range(4):
        lo = acc_ref[s * tq:(s + 1) * tq, :] / l_ref[s * tq:(s + 1) * tq, :]
        hi = acc_ref[(s + 4) * tq:(s + 5) * tq, :] / l_ref[(s + 4) * tq:(s + 5) * tq, :]
        o = jnp.where(lane < ATT_HEAD_DIM, lo, hi)
        az = az_ref[:, s * LANES:(s + 1) * LANES].astype(F32)
        ga_ref[:, s * LANES:(s + 1) * LANES] = (o * _silu(az)).astype(ga_ref.dtype)


def _attn_prompt(aq, iq, iw, az, akb, avb, ikk, tri, nb, t, topk):
    n = aq.shape[0]
    nq = t // Q_TILE
    qrow = lambda b, i: (b * nq + i, 0)
    seq = lambda b, i: (b, 0)
    return pl.pallas_call(
        functools.partial(_attn_prompt_kernel, topk=topk),
        grid=(nb, nq),
        in_specs=[pl.BlockSpec((Q_TILE, 512), qrow),
                  pl.BlockSpec((Q_TILE, 256), qrow),
                  pl.BlockSpec((Q_TILE, LANES), qrow),
                  pl.BlockSpec((Q_TILE, 512), qrow),
                  pl.BlockSpec((t, LANES), seq),
                  pl.BlockSpec((t, LANES), seq),
                  pl.BlockSpec((t, LANES), seq),
                  pl.BlockSpec((KEY_CHUNK, KEY_CHUNK), lambda b, i: (0, 0))],
        out_specs=pl.BlockSpec((Q_TILE, 512), qrow),
        out_shape=jax.ShapeDtypeStruct((n, 512), aq.dtype),
        scratch_shapes=[pltpu.VMEM((Q_TILE, t), I32),
                        pltpu.VMEM((ATT_HEADS * Q_TILE, LANES), F32),
                        pltpu.VMEM((ATT_HEADS * Q_TILE, LANES), F32),
                        pltpu.VMEM((ATT_HEADS * Q_TILE, LANES), F32)],
        compiler_params=pltpu.CompilerParams(dimension_semantics=("arbitrary", "arbitrary"),
                                             vmem_limit_bytes=VMEM_LIMIT),
        name="attn_prompt",
    )(aq, iq, iw, az, akb, avb, ikk, tri)


def _sample_scores_kernel(pt_ref, iq_ref, iw_ref, iknew_ref, *rest, n_pages, page, ts):
    del pt_ref
    page_refs, keys_ref = rest[:n_pages], rest[n_pages]
    past = n_pages * page
    iq = iq_ref[...].astype(F32)
    heads = [iq[:, h * IDX_DIM:(h + 1) * IDX_DIM] for h in range(IDX_HEADS)]
    iqs = jnp.concatenate(heads, axis=0).astype(BF16)
    iw = iw_ref[...] * (IDX_HEADS ** -0.5)
    iwb = [jnp.broadcast_to(iw[:, h:h + 1], (ts, page)) for h in range(IDX_HEADS)]
    for p in range(n_pages):
        sc = _indexer_scores(iqs, page_refs[p][...].astype(BF16), iwb, ts, feature_major=True)
        keys_ref[:, p * page:(p + 1) * page] = _score_keys(sc)
    new = jnp.concatenate([iknew_ref[...], jnp.zeros((page - ts, IDX_DIM), F32)], axis=0)
    sc = _indexer_scores(iqs, new.astype(BF16), iwb, ts)
    row = lax.broadcasted_iota(I32, (ts, page), 0)
    col = lax.broadcasted_iota(I32, (ts, page), 1)
    sc = jnp.where(col <= row, sc, -jnp.inf)
    keys_ref[:, past:past + page] = _score_keys(sc)


def _sample_scores(page_table, iq, iw, ik_new, cache_idx, layer, ts):
    nb, n_pages = page_table.shape
    page = cache_idx.shape[3]
    lp = (n_pages + 1) * page
    row = lambda b, pt: (b, 0)
    page_specs = [pl.BlockSpec((None, None, IDX_DIM, page),
                               functools.partial(lambda b, pt, p: (layer, pt[b, p], 0, 0), p=p))
                  for p in range(n_pages)]
    grid_spec = pltpu.PrefetchScalarGridSpec(
        num_scalar_prefetch=1,
        grid=(nb,),
        in_specs=[pl.BlockSpec((ts, 256), row),
                  pl.BlockSpec((ts, LANES), row),
                  pl.BlockSpec((ts, IDX_DIM), row)] + page_specs,
        out_specs=pl.BlockSpec((ts, lp), row),
    )
    return pl.pallas_call(
        functools.partial(_sample_scores_kernel, n_pages=n_pages, page=page, ts=ts),
        grid_spec=grid_spec,
        out_shape=jax.ShapeDtypeStruct((nb * ts, lp), I32),
        compiler_params=pltpu.CompilerParams(dimension_semantics=("arbitrary",),
                                             vmem_limit_bytes=VMEM_LIMIT),
        name="sample_scores",
    )(page_table, iq, iw, ik_new, *([cache_idx] * n_pages))


def _sample_select_kernel(keys_ref, tri_ref, bias_ref, *, topk, ts, past, lp):
    rows = keys_ref.shape[0]
    nch = lp // LANES
    tau, need = _kth_largest(keys_ref, nch, LANES, rows, topk)
    qpos = past + lax.rem(lax.broadcasted_iota(I32, (rows, LANES), 0), ts)
    col = lax.broadcasted_iota(I32, (rows, LANES), 1)
    tri = tri_ref[...]

    def body(c, tie_carry):
        k0 = pl.multiple_of(c * LANES, LANES)
        kk = keys_ref[:, pl.ds(k0, LANES)]
        bias, tie_carry = _select_bias(kk, tau, need, tie_carry, tri, k0 + col <= qpos, LANES, rows)
        bias_ref[:, pl.ds(k0, LANES)] = bias[0]
        return tie_carry

    lax.fori_loop(0, nch, body, jnp.zeros((rows, LANES), F32))


def _sample_select(keys, tri, topk, ts, past):
    n, lp = keys.shape
    rows = 128
    return pl.pallas_call(
        functools.partial(_sample_select_kernel, topk=topk, ts=ts, past=past, lp=lp),
        grid=(n // rows,),
        in_specs=[pl.BlockSpec((rows, lp), lambda i: (i, 0)),
                  pl.BlockSpec((LANES, LANES), lambda i: (0, 0))],
        out_specs=pl.BlockSpec((rows, lp), lambda i: (i, 0)),
        out_shape=jax.ShapeDtypeStruct((n, lp), F32),
        compiler_params=pltpu.CompilerParams(dimension_semantics=("arbitrary",),
                                             vmem_limit_bytes=VMEM_LIMIT),
        name="sample_select",
    )(keys, tri)


def _sample_attn_kernel(pt_ref, aq_ref, az_ref, bias_ref, knew_ref, vnew_ref, *rest,
                        n_pages, page, ts):
    del pt_ref
    k_refs, v_refs = rest[:n_pages], rest[n_pages:2 * n_pages]
    ga_ref, s_scr = rest[2 * n_pages], rest[2 * n_pages + 1]
    rows = ATT_HEADS * ts
    aq = aq_ref[...].astype(F32) * (ATT_HEAD_DIM ** -0.5)
    lo_heads, hi_heads = [], []
    for s in range(4):
        lo, hi = _split_heads(aq[:, s * LANES:(s + 1) * LANES], ts)
        lo_heads.append(lo)
        hi_heads.append(hi)
    qs = jnp.concatenate(lo_heads + hi_heads, axis=0).astype(BF16)
    pad = jnp.zeros((page - ts, LANES), F32)
    k_new = jnp.concatenate([knew_ref[...], pad], axis=0).astype(BF16)
    v_new = jnp.concatenate([vnew_ref[...], pad], axis=0).astype(BF16)

    def qk(c):
        if c < n_pages:
            return jnp.dot(qs, k_refs[c][...].astype(BF16), preferred_element_type=F32)
        return lax.dot_general(qs, k_new, NT_DIMS, preferred_element_type=F32)

    def pv(p, c):
        if c < n_pages:
            return lax.dot_general(p, v_refs[c][...].astype(BF16), NT_DIMS,
                                   preferred_element_type=F32)
        return jnp.dot(p, v_new, preferred_element_type=F32)

    mx = jnp.full((rows, LANES), NEG_BIG, F32)
    for c in range(n_pages + 1):
        sl = slice(c * page, (c + 1) * page)
        s = qk(c) + jnp.concatenate([bias_ref[:, sl]] * ATT_HEADS, axis=0)
        s_scr[:, sl] = s
        mx = jnp.maximum(mx, s)
    m = jnp.broadcast_to(jnp.max(mx, axis=1, keepdims=True), (rows, LANES))
    acc = jnp.zeros((rows, LANES), F32)
    psum = jnp.zeros((rows, LANES), F32)
    for c in range(n_pages + 1):
        p = jnp.exp(s_scr[:, c * page:(c + 1) * page] - m)
        psum = psum + p
        acc = acc + pv(p.astype(BF16), c)
    o = acc / jnp.broadcast_to(jnp.sum(psum, axis=1, keepdims=True), (rows, LANES))
    lane = lax.broadcasted_iota(I32, (ts, LANES), 1)
    for s in range(4):
        oo = jnp.where(lane < ATT_HEAD_DIM, o[s * ts:(s + 1) * ts], o[(s + 4) * ts:(s + 5) * ts])
        az = az_ref[:, s * LANES:(s + 1) * LANES].astype(F32)
        ga_ref[:, s * LANES:(s + 1) * LANES] = (oo * _silu(az)).astype(ga_ref.dtype)


def _sample_attn(page_table, aq, az, bias, k_new, v_new, cache_k, cache_v, layer, ts):
    nb, n_pages = page_table.shape
    page = cache_k.shape[3]
    lp = (n_pages + 1) * page
    row = lambda b, pt: (b, 0)
    page_specs = [pl.BlockSpec((None, None, LANES, page),
                               functools.partial(lambda b, pt, p: (layer, pt[b, p], 0, 0), p=p))
                  for p in range(n_pages)]
    grid_spec = pltpu.PrefetchScalarGridSpec(
        num_scalar_prefetch=1,
        grid=(nb,),
        in_specs=[pl.BlockSpec((ts, 512), row),
                  pl.BlockSpec((ts, 512), row),
                  pl.BlockSpec((ts, lp), row),
                  pl.BlockSpec((ts, LANES), row),
                  pl.BlockSpec((ts, LANES), row)] + page_specs + page_specs,
        out_specs=pl.BlockSpec((ts, 512), row),
        scratch_shapes=[pltpu.VMEM((ATT_HEADS * ts, lp), F32)],
    )
    return pl.pallas_call(
        functools.partial(_sample_attn_kernel, n_pages=n_pages, page=page, ts=ts),
        grid_spec=grid_spec,
        out_shape=jax.ShapeDtypeStruct((nb * ts, 512), aq.dtype),
        compiler_params=pltpu.CompilerParams(dimension_semantics=("arbitrary",),
                                             vmem_limit_bytes=VMEM_LIMIT),
        name="sample_attn",
    )(page_table, aq, az, bias, k_new, v_new, *([cache_k] * n_pages), *([cache_v] * n_pages))


def _merge_kernel(x_ref, p_ref, gr_ref, ga_ref, br_ref, ba_ref, wor_ref, woa_ref, wout_ref,
                  wpg_ref, wpp_ref, y_ref):
    u_r = jnp.dot(br_ref[...].astype(BF16), wor_ref[...], preferred_element_type=F32)
    u_a = jnp.dot(ba_ref[...].astype(BF16), woa_ref[...], preferred_element_type=F32)
    m = (jax.nn.sigmoid(gr_ref[...].astype(F32)) * u_r
         + jax.nn.sigmoid(ga_ref[...].astype(F32)) * u_a)
    x1 = x_ref[...] + jnp.dot(m.astype(BF16), wout_ref[...], preferred_element_type=F32)
    gate = jax.nn.sigmoid(jnp.dot(x1.astype(BF16), wpg_ref[...], preferred_element_type=F32))
    y_ref[...] = x1 + gate * jnp.dot(p_ref[...].astype(BF16), wpp_ref[...],
                                     preferred_element_type=F32)


def _merge(x, p, gr, ga, br, ba, wor, woa, wout, wpg, wpp, tm):
    n, d = x.shape
    row = lambda i: (i, 0)
    const = lambda i: (0, 0)
    return pl.pallas_call(
        _merge_kernel,
        grid=(n // tm,),
        in_specs=[pl.BlockSpec((tm, d), row),
                  pl.BlockSpec((tm, p.shape[1]), row),
                  pl.BlockSpec((tm, d), row),
                  pl.BlockSpec((tm, d), row),
                  pl.BlockSpec((tm, 512), row),
                  pl.BlockSpec((tm, 512), row),
                  pl.BlockSpec(wor.shape, const),
                  pl.BlockSpec(woa.shape, const),
                  pl.BlockSpec(wout.shape, const),
                  pl.BlockSpec(wpg.shape, const),
                  pl.BlockSpec(wpp.shape, const)],
        out_specs=pl.BlockSpec((tm, d), row),
        out_shape=jax.ShapeDtypeStruct((n, d), F32),
        compiler_params=pltpu.CompilerParams(dimension_semantics=("arbitrary",),
                                             vmem_limit_bytes=VMEM_LIMIT),
        name="merge",
    )(x, p, gr, ga, br, ba, wor, woa, wout, wpg, wpp)


def _pair_heads_rows(m):
    d = m.shape[1]
    m = m.reshape(ATT_KV_HEADS, ATT_HEADS // ATT_KV_HEADS, ATT_HEAD_DIM, d)
    return jnp.concatenate([m[0], m[1]], axis=1).reshape(ATT_HEADS * ATT_HEAD_DIM, d)


def _pack_w_in(wt):
    sizes = (512, 512, 512, 512, 512, 128, 128, 512, 256, 64, 4, 1024, 1024)
    offs, o = [], 0
    for s in sizes:
        offs.append((o, o + s))
        o += s
    rq, rk, rv, rz, aq, ak, av, az, iq, ik, iw, gr, ga = [wt[a:b] for a, b in offs]
    iwp = jnp.pad(iw, ((0, LANES - IDX_HEADS), (0, 0)))
    packed = jnp.concatenate([rq, rk, rv, rz, _pair_heads_rows(aq), _pair_heads_rows(az), ak, av,
                              iq, ik, ik, iwp, gr, ga], axis=0)
    return packed.astype(BF16)


def _rope_tables(pos):
    pf = pos.astype(F32)[:, None]
    half = RET_DK // 2
    freqs = jnp.exp(-math.log(RET_THETA) * jnp.arange(half, dtype=F32) / half)
    ang = pf * freqs[None, :]
    cos, sin = jnp.cos(ang), jnp.sin(ang)
    rc = jnp.concatenate([cos, cos], axis=1)
    rs = jnp.concatenate([-sin, sin], axis=1)
    half = ROPE_DIM // 2
    freqs = jnp.exp(-math.log(ROPE_THETA) * jnp.arange(half, dtype=F32) / half)
    ang = pf * freqs[None, :]
    cos, sin = jnp.cos(ang), jnp.sin(ang)
    r = pos.shape[0]
    rest = ATT_HEAD_DIM - ROPE_DIM
    one, zero, zh = jnp.ones((r, rest), F32), jnp.zeros((r, rest), F32), jnp.zeros((r, half), F32)
    ac = jnp.concatenate([cos, cos, one], axis=1)
    as1 = jnp.concatenate([-sin, zh, zero], axis=1)
    as2 = jnp.concatenate([zh, sin, zero], axis=1)
    tile2 = lambda a: jnp.concatenate([a, a], axis=1)
    return rc, rs, tile2(ac), tile2(as1), tile2(as2)


def _decay_tables(c_eff):
    h = RET_HEADS
    log_g = jnp.log1p(-jnp.exp2(-5.0 - jnp.arange(h, dtype=F32)))
    c = jnp.arange(RET_CHUNK, dtype=F32)
    diff = c[:, None] - c[None, :]
    di = jnp.where(diff[None] >= 0, jnp.exp(jnp.maximum(diff, 0.0)[None] * log_g[:, None, None]), 0.0)
    dq = jnp.exp((c[:, None] + 1.0) * log_g[None, :])
    dk = jnp.exp((c_eff - 1.0 - c)[:, None] * log_g[None, :])
    dk = jnp.where(c[:, None] < c_eff, dk, 0.0)
    ds = jnp.exp(c_eff * log_g)
    rep = lambda a: jnp.repeat(a, RET_DK, axis=-1)
    return di, rep(dq), rep(dk), rep(ds[None, :])


def _tri(n):
    r = lax.broadcasted_iota(I32, (n, n), 0)
    c = lax.broadcasted_iota(I32, (n, n), 1)
    return jnp.where(r <= c, 1.0, 0.0).astype(BF16)


def _group_sum_matrix():
    r = lax.broadcasted_iota(I32, (LANES, LANES), 0) // ATT_HEAD_DIM
    c = lax.broadcasted_iota(I32, (LANES, LANES), 1) // ATT_HEAD_DIM
    return jnp.where(r == c, 1.0, 0.0).astype(BF16)


def _layer_weights(i, norm_gain, w_in_t, q_norm_gain, k_norm_gain, w_o_ret, w_o_att, w_out,
                   w_ple_gate, w_ple_proj):
    woa = _pair_heads_rows(w_o_att[i])
    return dict(
        gain=norm_gain[i][None, :],
        w_in=_pack_w_in(w_in_t[:, i, :]),
        qg=jnp.tile(q_norm_gain[i], 2)[None, :],
        kg=jnp.tile(k_norm_gain[i], 2)[None, :],
        wor=w_o_ret[i].astype(BF16),
        woa=woa.astype(BF16),
        wout=w_out[i].astype(BF16),
        wpg=w_ple_gate[i].astype(BF16),
        wpp=w_ple_proj[i].astype(BF16),
    )


def kernel(x_prompt, x_sample, cache_k, cache_v, cache_idx_k, state_ret, page_table, p_prompt,
           p_sample, norm_gain, w_in, q_norm_gain, k_norm_gain, w_o_ret, w_o_att, w_out, w_ple_gate,
           w_ple_proj):
    bp, tp, d = x_prompt.shape
    bs, ts, _ = x_sample.shape
    depth = w_in.shape[0]
    n_pool, page = cache_k.shape[1], cache_k.shape[2]
    n_pages = page_table.shape[1]
    past = n_pages * page
    topk_p = min(TOPK_MAX, tp // 4)
    topk_s = min(TOPK_MAX, (past + ts) // 4)
    assert tp % KEY_CHUNK == 0 and tp % RET_CHUNK == 0 and ts <= page and ts % 8 == 0

    np_, ns_ = bp * tp, bs * ts
    tm_p = 512 if np_ % 512 == 0 else Q_TILE
    tm_s = 256 if ns_ % 256 == 0 else ns_
    assert tm_s % ts == 0 and tp % tm_p == 0

    gsum = _group_sum_matrix()
    tabs_p = _rope_tables(jnp.arange(tp, dtype=I32))
    tabs_s = _rope_tables(past + (jnp.arange(tm_s, dtype=I32) % ts))
    decay_p = _decay_tables(float(RET_CHUNK))
    decay_s = _decay_tables(float(ts))
    tri_p, tri_s = _tri(KEY_CHUNK), _tri(LANES)
    ck = cache_k.reshape(depth, n_pool, page, LANES).transpose(0, 1, 3, 2)
    cv = cache_v.reshape(depth, n_pool, page, LANES).transpose(0, 1, 3, 2)
    cik = cache_idx_k.transpose(0, 1, 3, 2)
    w_in_t = w_in.transpose(2, 0, 1)
    s0_p = jnp.zeros((bp, RET_HEADS, RET_DK, RET_DV), F32)

    xp = x_prompt.reshape(np_, d)
    xs = x_sample.reshape(ns_, d)
    outs = {k: [] for k in ("kp", "vp", "ikp", "sp", "ks", "vs", "iks", "ss")}

    def pad_rows(a):
        w = a.shape[1]
        return jnp.pad(a.reshape(bs, ts, w), ((0, 0), (0, RET_CHUNK - ts), (0, 0))).reshape(
            bs * RET_CHUNK, w)

    for i in range(depth):
        lw = _layer_weights(i, norm_gain, w_in_t, q_norm_gain, k_norm_gain, w_o_ret, w_o_att, w_out,
                            w_ple_gate, w_ple_proj)

        (rq, rk, rv, rz, aq, az, akb, avb, iq, ikk, gr, ga, ak, av, ik, iw) = _inproj(
            xp, lw["gain"], lw["w_in"], tabs_p, lw["qg"], lw["kg"], gsum, tm_p, tp // tm_p, BF16)
        b_r, s_new = _retention(rq, rk, rv, rz, s0_p, decay_p, bp, tp // RET_CHUNK, BF16)
        b_a = _attn_prompt(aq, iq, iw, az, akb, avb, ikk, tri_p, bp, tp, topk_p)
        xp = _merge(xp, p_prompt[i].reshape(np_, -1), gr, ga, b_r, b_a, lw["wor"], lw["woa"],
                    lw["wout"], lw["wpg"], lw["wpp"], tm_p)
        outs["kp"].append(ak.reshape(bp, tp, ATT_KV_HEADS, ATT_HEAD_DIM))
        outs["vp"].append(av.reshape(bp, tp, ATT_KV_HEADS, ATT_HEAD_DIM))
        outs["ikp"].append(ik.reshape(bp, tp, IDX_DIM))
        outs["sp"].append(s_new)

        (rq, rk, rv, rz, aq, az, akb, avb, iq, ikk, gr, ga, ak, av, ik, iw) = _inproj(
            xs, lw["gain"], lw["w_in"], tabs_s, lw["qg"], lw["kg"], gsum, tm_s, 1, F32)
        b_r, s_new = _retention(pad_rows(rq), pad_rows(rk), pad_rows(rv), pad_rows(rz),
                                state_ret[i], decay_s, bs, 1, F32)
        b_r = b_r.reshape(bs, RET_CHUNK, 512)[:, :ts].reshape(ns_, 512)
        keys = _sample_scores(page_table, iq, iw, ik, cik, i, ts)
        bias = _sample_select(keys, tri_s, topk_s, ts, past)
        b_a = _sample_attn(page_table, aq, az, bias, ak, av, ck, cv, i, ts)
        xs = _merge(xs, p_sample[i].reshape(ns_, -1), gr, ga, b_r, b_a, lw["wor"], lw["woa"],
                    lw["wout"], lw["wpg"], lw["wpp"], tm_s)
        outs["ks"].append(ak.reshape(bs, ts, ATT_KV_HEADS, ATT_HEAD_DIM))
        outs["vs"].append(av.reshape(bs, ts, ATT_KV_HEADS, ATT_HEAD_DIM))
        outs["iks"].append(ik.reshape(bs, ts, IDX_DIM))
        outs["ss"].append(s_new)

    st = lambda k: jnp.stack(outs[k])
    return (xp.reshape(bp, tp, d), xs.reshape(bs, ts, d), st("kp"), st("vp"), st("ikp"), st("sp"),
            st("ks"), st("vs"), st("iks"), st("ss"))
```

```python
import functools
import math

import jax
import jax.numpy as jnp
from jax import lax
from jax.experimental import pallas as pl
from jax.experimental.pallas import tpu as pltpu

F32 = jnp.float32
BF16 = jnp.bfloat16
I32 = jnp.int32

RET_HEADS = 4
RET_DK = 128
RET_DV = 128
RET_CHUNK = 128
RET_THETA = 10000.0
ATT_HEADS = 8
ATT_KV_HEADS = 2
ATT_HEAD_DIM = 64
ROPE_THETA = 500000.0
ROPE_DIM = ATT_HEAD_DIM // 4
IDX_HEADS = 4
IDX_DIM = 64
TOPK_MAX = 256
NORM_EPS = 1e-6
GN_EPS = 1e-5

LANES = 128
Q_TILE = 128
KEY_CHUNK = 512
VMEM_LIMIT = 56 * 1024 * 1024
NEG_BIG = -1e30
INT_MIN = -(2 ** 31)

C_RQ, C_RK, C_RV, C_RZ = 0, 512, 1024, 1536
C_AQ, C_AZ, C_AK, C_AV = 2048, 2560, 3072, 3200
C_IQ, C_IKK, C_IW, C_GR, C_GA = 3328, 3584, 3712, 3840, 4864
W_PACKED = 5888

NT_DIMS = (((1,), (1,)), ((), ()))
VT_ROWS = LANES + 16
LOG2E = math.log2(math.e)


def _silu(x):
    return x * jax.nn.sigmoid(x)


def _inproj_kernel(x_ref, g_ref, w_ref, rc_ref, rs_ref, ac_ref, as1_ref, as2_ref, qg_ref, kg_ref,
                   gsum_ref,
                   rq_o, rk_o, rv_o, rz_o, aq_o, az_o, akb_o, avt_o, iq_o, ikk_o, gr_o, ga_o,
                   ak_o, av_o, ik_o, iw_o):
    sd = rq_o.dtype
    x = x_ref[...]
    ms = jnp.mean(x * x, axis=-1, keepdims=True)
    hb = ((x * lax.rsqrt(ms + NORM_EPS)) * g_ref[...]).astype(BF16)

    def mm(c0, width):
        return lax.dot_general(hb, w_ref[c0:c0 + width, :], NT_DIMS, preferred_element_type=F32)

    rc, rs = rc_ref[...], rs_ref[...]
    ac, as1, as2 = ac_ref[...], as1_ref[...], as2_ref[...]
    gsum = gsum_ref[...]

    def rope_ret(z):
        return z * rc + pltpu.roll(z, 64, 1) * rs

    def rope_att(z):
        return z * ac + pltpu.roll(z, LANES - 8, 1) * as1 + pltpu.roll(z, 8, 1) * as2

    def head_norm(z, gain):
        sq = z * z
        hi = sq.astype(BF16)
        lo = (sq - hi.astype(F32)).astype(BF16)
        ssq = (jnp.dot(hi, gsum, preferred_element_type=F32)
               + jnp.dot(lo, gsum, preferred_element_type=F32))
        return (z * lax.rsqrt(ssq * (1.0 / ATT_HEAD_DIM) + NORM_EPS)) * gain

    def slab(z, s):
        return z[:, s * LANES:(s + 1) * LANES]

    z = mm(C_RQ, 512)
    for s in range(4):
        rq_o[:, s * LANES:(s + 1) * LANES] = rope_ret(slab(z, s)).astype(sd)
    z = mm(C_RK, 512)
    for s in range(4):
        rk_o[:, s * LANES:(s + 1) * LANES] = (rope_ret(slab(z, s)) * (RET_DK ** -0.5)).astype(sd)
    rv_o[...] = mm(C_RV, 512).astype(sd)
    rz_o[...] = mm(C_RZ, 512).astype(sd)

    qg, kg = qg_ref[...], kg_ref[...]
    z = mm(C_AQ, 512)
    for s in range(4):
        aq_o[:, s * LANES:(s + 1) * LANES] = rope_att(head_norm(slab(z, s), qg)).astype(sd)
    az_o[...] = mm(C_AZ, 512).astype(sd)

    z = mm(C_AK, 256)
    k = rope_att(head_norm(slab(z, 0), kg))
    ak_o[...] = k
    akb_o[...] = k.astype(BF16)
    v = slab(z, 1)
    av_o[...] = v
    avt_o[...] = jnp.concatenate([v.T, jnp.ones((VT_ROWS - LANES, v.shape[0]), F32)],
                                 axis=0).astype(BF16)

    z = mm(C_IQ, 512)
    for s in range(2):
        iq_o[:, s * LANES:(s + 1) * LANES] = rope_att(slab(z, s)).astype(sd)
    ikk = rope_att(slab(z, 2))
    ikk_o[...] = ikk.astype(BF16)
    ik_o[...] = ikk[:, :IDX_DIM]
    iw_o[...] = slab(z, 3)

    gr_o[...] = mm(C_GR, 1024).astype(sd)
    ga_o[...] = mm(C_GA, 1024).astype(sd)


def _inproj(x, gain, w_packed, tabs, qg, kg, gsum, tm, pos_period_tiles, sd):
    n, d = x.shape
    grid = (n // tm,)
    row = lambda i: (i, 0)
    const = lambda i: (0, 0)
    tab = lambda i: (i % pos_period_tiles, 0)
    in_specs = [
        pl.BlockSpec((tm, d), row),
        pl.BlockSpec((1, d), const),
        pl.BlockSpec((W_PACKED, d), const),
    ] + [pl.BlockSpec((tm, LANES), tab)] * 5 + [
        pl.BlockSpec((1, LANES), const),
        pl.BlockSpec((1, LANES), const),
        pl.BlockSpec((LANES, LANES), const),
    ]
    widths = [(512, sd)] * 6 + [(128, BF16), (128, BF16), (256, sd), (128, BF16), (1024, sd), (1024, sd),
                                (128, F32), (128, F32), (IDX_DIM, F32), (128, F32)]
    out_shape = [jax.ShapeDtypeStruct((n, w), dt) for w, dt in widths]
    out_specs = [pl.BlockSpec((tm, w), row) for w, _ in widths]
    out_shape[7] = jax.ShapeDtypeStruct((VT_ROWS, n), BF16)
    out_specs[7] = pl.BlockSpec((VT_ROWS, tm), lambda i: (0, i))
    return pl.pallas_call(
        _inproj_kernel,
        grid=grid,
        in_specs=in_specs,
        out_specs=out_specs,
        out_shape=out_shape,
        compiler_params=pltpu.CompilerParams(dimension_semantics=("arbitrary",),
                                             vmem_limit_bytes=VMEM_LIMIT),
        name="inproj",
    )(x, gain, w_packed, *tabs, qg, kg, gsum)


def _retention_kernel(q_ref, k_ref, v_ref, z_ref, s0_ref, di_ref, dq_ref, dk_ref, ds_ref,
                      o_ref, sout_ref, s_scr, *, group):
    c = pl.program_id(1)
    nc = pl.num_programs(1)

    @pl.when(c == 0)
    def _():
        s_scr[...] = s0_ref[...]

    for g in range(group):
        for h in range(RET_HEADS):
            hs = slice(h * LANES, (h + 1) * LANES)
            q = q_ref[g, :, hs].astype(F32)
            k = k_ref[g, :, hs].astype(F32)
            v = v_ref[g, :, hs].astype(BF16)
            s_old = s_scr[g, h]
            inner = lax.dot_general(q.astype(BF16), k.astype(BF16), NT_DIMS,
                                    preferred_element_type=F32) * di_ref[h]
            o = (jnp.dot(inner.astype(BF16), v, preferred_element_type=F32)
                 + jnp.dot((q * dq_ref[:, hs]).astype(BF16), s_old.astype(BF16),
                           preferred_element_type=F32))
            kd = (k * dk_ref[:, hs]).T.astype(BF16)
            s_scr[g, h] = s_old * ds_ref[:, hs] + jnp.dot(kd, v, preferred_element_type=F32)
            mu = jnp.mean(o, axis=-1, keepdims=True)
            cen = o - mu
            var = jnp.mean(cen * cen, axis=-1, keepdims=True)
            gn = cen * lax.rsqrt(var + GN_EPS)
            o_ref[g, :, hs] = (gn * _silu(z_ref[g, :, hs].astype(F32))).astype(o_ref.dtype)

    @pl.when(c == nc - 1)
    def _():
        sout_ref[...] = s_scr[...]


def _retention(rq, rk, rv, rz, s0, decay, nb, nchunks, out_dtype):
    di, dq, dk, ds = decay
    c = RET_CHUNK
    group = max(g for g in (4, 2, 1) if nb % g == 0)
    blk = pl.BlockSpec((group, c, 512), lambda b, j: (b, j, 0))
    st = pl.BlockSpec((group, RET_HEADS, RET_DK, RET_DV), lambda b, j: (b, 0, 0, 0))
    const2 = lambda b, j: (0, 0)
    return pl.pallas_call(
        functools.partial(_retention_kernel, group=group),
        grid=(nb // group, nchunks),
        in_specs=[blk, blk, blk, blk, st,
                  pl.BlockSpec((RET_HEADS, c, c), lambda b, j: (0, 0, 0)),
                  pl.BlockSpec((c, 512), const2),
                  pl.BlockSpec((c, 512), const2),
                  pl.BlockSpec((1, 512), const2)],
        out_specs=[blk, st],
        out_shape=[jax.ShapeDtypeStruct((nb, nchunks * c, 512), out_dtype),
                   jax.ShapeDtypeStruct((nb, RET_HEADS, RET_DK, RET_DV), F32)],
        scratch_shapes=[pltpu.VMEM((group, RET_HEADS, RET_DK, RET_DV), F32)],
        compiler_params=pltpu.CompilerParams(dimension_semantics=("arbitrary", "arbitrary"),
                                             vmem_limit_bytes=VMEM_LIMIT),
        name="retention",
    )(rq, rk, rv, rz, s0, di, dq, dk, ds)


KEY_NEG_INF = -2139095041


def _cand_float(key):
    bits = key ^ (lax.shift_right_arithmetic(key, 31) & 0x7FFFFFFF)
    return jnp.where(key < KEY_NEG_INF, -jnp.inf, pltpu.bitcast(bits, F32))


def _split_heads(x2, rows):
    lane = lax.broadcasted_iota(I32, (rows, LANES), 1)
    lo = lane < ATT_HEAD_DIM
    return jnp.where(lo, x2, 0.0), jnp.where(lo, 0.0, x2)


def _indexer_scores(iqs, ikc, iwb, rows, feature_major=False):
    if feature_major:
        lg = jnp.dot(iqs, ikc, preferred_element_type=F32)
    else:
        lg = lax.dot_general(iqs, ikc, NT_DIMS, preferred_element_type=F32)
    sc = None
    for h in range(IDX_HEADS):
        t = jnp.maximum(lg[h * rows:(h + 1) * rows] * (IDX_DIM ** -0.5), 0.0) * iwb[h]
        sc = t if sc is None else sc + t
    return sc


def _count(sc_ref, nchunks, kc, rows, pred):
    def body(c, acc):
        k0 = pl.multiple_of(c * kc, kc)
        kk = sc_ref[:, pl.ds(k0, kc)]
        for j in range(kc // LANES):
            acc = acc + jnp.where(pred(kk[:, j * LANES:(j + 1) * LANES]), 1.0, 0.0)
        return acc
    acc = lax.fori_loop(0, nchunks, body, jnp.zeros((rows, LANES), F32))
    return jnp.broadcast_to(jnp.sum(acc, axis=1, keepdims=True), (rows, LANES))


def _kth_largest(sc_ref, nchunks, kc, rows, topk):
    kf = float(topk)
    c0 = _count(sc_ref, nchunks, kc, rows, lambda kk: kk >= 0.0)
    key = jnp.where(c0 >= kf, 0, INT_MIN).astype(I32)

    def bitstep(b, key):
        cand = key | lax.shift_left(jnp.int32(1), 30 - b)
        cf = _cand_float(cand)
        cnt = _count(sc_ref, nchunks, kc, rows, lambda kk: kk >= cf)
        return jnp.where(cnt >= kf, cand, key)

    tau = _cand_float(lax.fori_loop(0, 31, bitstep, key))
    n_gt = _count(sc_ref, nchunks, kc, rows, lambda kk: kk > tau)
    return tau, kf - n_gt


def _select_bias(kk, tau, need, tie_carry, tri, causal, kc, rows):
    nslab = kc // LANES
    eq = [kk[:, j * LANES:(j + 1) * LANES] == tau for j in range(nslab)]
    eqf = jnp.concatenate([jnp.where(e, 1.0, 0.0) for e in eq], axis=1).astype(BF16)
    rank = jnp.dot(eqf, tri, preferred_element_type=F32)
    total = jnp.dot(eqf, jnp.ones((kc, LANES), BF16), preferred_element_type=F32)
    bias = []
    for j in range(nslab):
        sl = slice(j * LANES, (j + 1) * LANES)
        tie_ok = jnp.where((rank[:, sl] + tie_carry) <= need, 0.0, NEG_BIG)
        b = jnp.where(kk[:, sl] > tau, 0.0, jnp.where(eq[j], tie_ok, NEG_BIG))
        bias.append(jnp.where(causal[:, sl], b, NEG_BIG))
    return bias, tie_carry + total


def _rows8(x):
    return x.reshape(x.shape[0] // 8, 8, LANES)


def _fold8(x8, op):
    return jnp.broadcast_to(op(x8, axis=0, keepdims=True), (8, LANES))


def _reduce0(x3, op, ways=8):
    accs = [x3[j] for j in range(ways)]
    for j in range(ways, x3.shape[0]):
        accs[j % ways] = op(accs[j % ways], x3[j])
    while len(accs) > 1:
        accs = [op(accs[a], accs[a + 1]) for a in range(0, len(accs), 2)]
    return accs[0]


I16 = jnp.int16
I16_MIN = -(2 ** 15)


def _count_f(sc_ref, nchunks, kc, cand, strict, ways=4):
    def body(c, accs):
        k0 = pl.multiple_of(c * kc, kc)
        x = _rows8(sc_ref[pl.ds(k0, kc), :])
        ones = jnp.where((x > cand[None]) if strict else (x >= cand[None]), 1.0, 0.0)
        accs = list(accs)
        for j in range(kc // 8):
            accs[j % ways] = accs[j % ways] + ones[j]
        return tuple(accs)

    accs = lax.fori_loop(0, nchunks, body, tuple(jnp.zeros((8, LANES), F32) for _ in range(ways)))
    return _fold8((accs[0] + accs[1]) + (accs[2] + accs[3]), jnp.sum)


def _count_b(sb_ref, nchunks, kc, cand, ways=4):
    c16 = jnp.concatenate([cand, cand], axis=0).astype(BF16)[None]

    def body(c, accs):
        k0 = pl.multiple_of(c * kc, kc)
        x = sb_ref[pl.ds(k0, kc), :].reshape(kc // 16, 16, LANES)
        ones = jnp.where(x >= c16, jnp.int16(1), jnp.int16(0))
        accs = list(accs)
        for j in range(kc // 16):
            accs[j % ways] = accs[j % ways] + ones[j]
        return tuple(accs)

    accs = lax.fori_loop(0, nchunks, body, tuple(jnp.zeros((16, LANES), I16) for _ in range(ways)))
    tot = ((accs[0] + accs[1]) + (accs[2] + accs[3])).astype(I32).astype(F32)
    return jnp.broadcast_to(jnp.sum(tot, axis=0, keepdims=True), (8, LANES))


def _kth_largest_t(sc_ref, sb_ref, nchunks, kc, topk):
    kf = float(topk)
    zero = jnp.zeros((8, LANES), I32)
    c0 = _count_b(sb_ref, nchunks, kc, _cand_float(zero))
    h = jnp.where(c0 >= kf, 0, I16_MIN).astype(I32)

    def hi_step(b, h):
        cand = h | lax.shift_left(jnp.int32(1), 14 - b)
        cnt = _count_b(sb_ref, nchunks, kc, _cand_float(cand * 65536))
        return jnp.where(cnt >= kf, cand, h)

    h = lax.fori_loop(0, 15, hi_step, h)
    base = jnp.maximum(h, I16_MIN + 1) * 65536 - 32768

    def lo_step(b, o):
        cand = o | lax.shift_left(jnp.int32(1), 16 - b)
        cnt = _count_f(sc_ref, nchunks, kc, _cand_float(base + cand), False)
        return jnp.where(cnt >= kf, cand, o)

    tau = _cand_float(base + lax.fori_loop(0, 17, lo_step, zero))
    return tau, kf - _count_f(sc_ref, nchunks, kc, tau, True)


def _attn_prompt_kernel(aq_ref, iq_ref, iw_ref, az_ref, kb_ref, vt_ref, ik_ref, tril_ref, ga_ref,
                        sc_ref, sb_ref, m_ref, acc_ref, *, topk):
    tq, kc = Q_TILE, KEY_CHUNK
    i = pl.program_id(1)
    nkc = lax.div(i * tq + tq + kc - 1, kc)
    qpos = i * tq + lax.broadcasted_iota(I32, (kc, tq), 1)
    krow = lax.broadcasted_iota(I32, (kc, tq), 0)

    iq = iq_ref[...].astype(F32) * (IDX_DIM ** -0.5)
    parts = []
    for s in range(2):
        parts += list(_split_heads(iq[:, s * LANES:(s + 1) * LANES], tq))
    iqs = jnp.concatenate(parts, axis=0).astype(BF16)
    iwt = (iw_ref[...] * (IDX_HEADS ** -0.5)).T

    def scores(c, carry):
        k0 = pl.multiple_of(c * kc, kc)
        ikc = ik_ref[pl.ds(k0, kc), :]
        sc = None
        for pr in range(IDX_HEADS // 2):
            lg = lax.dot_general(ikc, iqs[pr * 2 * tq:(pr + 1) * 2 * tq], NT_DIMS,
                                 preferred_element_type=F32)
            for hh in range(2):
                h = 2 * pr + hh
                t = jnp.maximum(lg[:, hh * tq:(hh + 1) * tq], 0.0) * iwt[h:h + 1, :]
                sc = t if sc is None else sc + t
        sc = jnp.where(k0 + krow <= qpos, sc, -jnp.inf)
        sc_ref[pl.ds(k0, kc), :] = sc
        sb_ref[pl.ds(k0, kc), :] = sc.astype(BF16)
        return carry

    lax.fori_loop(0, nkc, scores, 0)
    tau, need = _kth_largest_t(sc_ref, sb_ref, nkc, kc, topk)

    aq = aq_ref[...].astype(F32) * (ATT_HEAD_DIM ** -0.5 * LOG2E)
    lo_heads, hi_heads = [], []
    for s in range(4):
        lo, hi = _split_heads(aq[:, s * LANES:(s + 1) * LANES], tq)
        lo_heads.append(lo)
        hi_heads.append(hi)
    qs = jnp.concatenate(lo_heads + hi_heads, axis=0).astype(BF16)

    m_ref[...] = jnp.full(m_ref.shape, NEG_BIG, F32)
    acc_ref[...] = jnp.zeros(acc_ref.shape, F32)
    tril = tril_ref[...]
    nblk = kc // LANES

    def attend(c, tie_carry):
        k0 = pl.multiple_of(c * kc, kc)
        kk3 = _rows8(sc_ref[pl.ds(k0, kc), :])
        eqf = jnp.where(kk3 == tau[None], 1.0, 0.0).reshape(kc, LANES).astype(BF16)
        ranks = []
        for blk in range(nblk):
            r = jnp.dot(tril, eqf[blk * LANES:(blk + 1) * LANES], preferred_element_type=F32)
            ranks.append(_rows8(r) + tie_carry[None])
            tie_carry = tie_carry + jnp.broadcast_to(r[LANES - 1:LANES, :], (8, LANES))
        rank = jnp.concatenate(ranks, axis=0)
        tie_ok = jnp.where(rank <= need[None], 0.0, NEG_BIG)
        bias = jnp.where(kk3 > tau[None], 0.0, jnp.where(kk3 == tau[None], tie_ok, NEG_BIG))
        bias = jnp.where(_rows8(k0 + krow) <= _rows8(qpos), bias, NEG_BIG)

        kcb = kb_ref[pl.ds(k0, kc), :]
        p_cols, alphas = [], []
        for pr in range(ATT_HEADS // 2):
            s = lax.dot_general(kcb, qs[pr * 2 * tq:(pr + 1) * 2 * tq], NT_DIMS,
                                preferred_element_type=F32)
            for hh in range(2):
                h = 2 * pr + hh
                sh = _rows8(s[:, hh * tq:(hh + 1) * tq]) + bias
                m_old = m_ref[h]
                m_new = jnp.maximum(m_old, _fold8(_reduce0(sh, jnp.maximum), jnp.max))
                alpha = jnp.exp2(m_old - m_new)
                p = jnp.exp2(sh - m_new[None])
                m_ref[h] = m_new
                p_cols.append(p.reshape(kc, tq).astype(BF16))
                alphas.append(alpha[0:1, :])
        pv = jnp.dot(vt_ref[:, pl.ds(k0, kc)], jnp.concatenate(p_cols, axis=1),
                     preferred_element_type=F32)
        acc_ref[...] = jnp.concatenate(alphas, axis=1) * acc_ref[...] + pv
        return tie_carry

    lax.fori_loop(0, nkc, attend, jnp.zeros((8, LANES), F32))

    frow = lax.broadcasted_iota(I32, (LANES, tq), 0)

    def head_out(h):
        cols = slice(h * tq, (h + 1) * tq)
        return acc_ref[0:LANES, cols] / acc_ref[LANES:LANES + 1, cols]

    for s in range(4):
        o = jnp.where(frow < ATT_HEAD_DIM, head_out(s), head_out(s + 4)).T
        az = az_ref[:, s * LANES:(s + 1) * LANES].astype(F32)
        ga_ref[:, s * LANES:(s + 1) * LANES] = (o * _silu(az)).astype(ga_ref.dtype)


def _attn_prompt(aq, iq, iw, az, akb, avt, ikk, tril, nb, t, topk):
    n = aq.shape[0]
    nq = t // Q_TILE
    qrow = lambda b, i: (b * nq + i, 0)
    seq = lambda b, i: (b, 0)
    return pl.pallas_call(
        functools.partial(_attn_prompt_kernel, topk=topk),
        grid=(nb, nq),
        in_specs=[pl.BlockSpec((Q_TILE, 512), qrow),
                  pl.BlockSpec((Q_TILE, 256), qrow),
                  pl.BlockSpec((Q_TILE, LANES), qrow),
                  pl.BlockSpec((Q_TILE, 512), qrow),
                  pl.BlockSpec((t, LANES), seq),
                  pl.BlockSpec((VT_ROWS, t), lambda b, i: (0, b)),
                  pl.BlockSpec((t, LANES), seq),
                  pl.BlockSpec((LANES, LANES), lambda b, i: (0, 0))],
        out_specs=pl.BlockSpec((Q_TILE, 512), qrow),
        out_shape=jax.ShapeDtypeStruct((n, 512), aq.dtype),
        scratch_shapes=[pltpu.VMEM((t, Q_TILE), F32),
                        pltpu.VMEM((t, Q_TILE), BF16),
                        pltpu.VMEM((ATT_HEADS, 8, Q_TILE), F32),
                        pltpu.VMEM((VT_ROWS, ATT_HEADS * Q_TILE), F32)],
        compiler_params=pltpu.CompilerParams(dimension_semantics=("arbitrary", "arbitrary"),
                                             vmem_limit_bytes=VMEM_LIMIT),
        name="attn_prompt",
    )(aq, iq, iw, az, akb, avt, ikk, tril)


def _sample_scores_kernel(pt_ref, iq_ref, iw_ref, iknew_ref, *rest, n_pages, page, ts):
    del pt_ref
    page_refs, sc_ref = rest[:n_pages], rest[n_pages]
    past = n_pages * page
    iq = iq_ref[...].astype(F32)
    heads = [iq[:, h * IDX_DIM:(h + 1) * IDX_DIM] for h in range(IDX_HEADS)]
    iqs = jnp.concatenate(heads, axis=0).astype(BF16)
    iw = iw_ref[...] * (IDX_HEADS ** -0.5)
    iwb = [jnp.broadcast_to(iw[:, h:h + 1], (ts, page)) for h in range(IDX_HEADS)]
    for p in range(n_pages):
        sc_ref[:, p * page:(p + 1) * page] = _indexer_scores(
            iqs, page_refs[p][...].astype(BF16), iwb, ts, feature_major=True)
    new = jnp.concatenate([iknew_ref[...], jnp.zeros((page - ts, IDX_DIM), F32)], axis=0)
    sc = _indexer_scores(iqs, new.astype(BF16), iwb, ts)
    row = lax.broadcasted_iota(I32, (ts, page), 0)
    col = lax.broadcasted_iota(I32, (ts, page), 1)
    sc_ref[:, past:past + page] = jnp.where(col <= row, sc, -jnp.inf)


def _sample_scores(page_table, iq, iw, ik_new, cache_idx, layer, ts):
    nb, n_pages = page_table.shape
    page = cache_idx.shape[3]
    lp = (n_pages + 1) * page
    row = lambda b, pt: (b, 0)
    page_specs = [pl.BlockSpec((None, None, IDX_DIM, page),
                               functools.partial(lambda b, pt, p: (layer, pt[b, p], 0, 0), p=p))
                  for p in range(n_pages)]
    grid_spec = pltpu.PrefetchScalarGridSpec(
        num_scalar_prefetch=1,
        grid=(nb,),
        in_specs=[pl.BlockSpec((ts, 256), row),
                  pl.BlockSpec((ts, LANES), row),
                  pl.BlockSpec((ts, IDX_DIM), row)] + page_specs,
        out_specs=pl.BlockSpec((ts, lp), row),
    )
    return pl.pallas_call(
        functools.partial(_sample_scores_kernel, n_pages=n_pages, page=page, ts=ts),
        grid_spec=grid_spec,
        out_shape=jax.ShapeDtypeStruct((nb * ts, lp), F32),
        compiler_params=pltpu.CompilerParams(dimension_semantics=("arbitrary",),
                                             vmem_limit_bytes=VMEM_LIMIT),
        name="sample_scores",
    )(page_table, iq, iw, ik_new, *([cache_idx] * n_pages))


def _sample_select_kernel(sc_ref, tri_ref, bias_ref, *, topk, ts, past, lp):
    rows = sc_ref.shape[0]
    nch = lp // LANES
    tau, need = _kth_largest(sc_ref, nch, LANES, rows, topk)
    qpos = past + lax.rem(lax.broadcasted_iota(I32, (rows, LANES), 0), ts)
    col = lax.broadcasted_iota(I32, (rows, LANES), 1)
    tri = tri_ref[...]

    def body(c, tie_carry):
        k0 = pl.multiple_of(c * LANES, LANES)
        kk = sc_ref[:, pl.ds(k0, LANES)]
        bias, tie_carry = _select_bias(kk, tau, need, tie_carry, tri, k0 + col <= qpos, LANES, rows)
        bias_ref[:, pl.ds(k0, LANES)] = bias[0]
        return tie_carry

    lax.fori_loop(0, nch, body, jnp.zeros((rows, LANES), F32))


def _sample_select(keys, tri, topk, ts, past):
    n, lp = keys.shape
    rows = 128
    return pl.pallas_call(
        functools.partial(_sample_select_kernel, topk=topk, ts=ts, past=past, lp=lp),
        grid=(n // rows,),
        in_specs=[pl.BlockSpec((rows, lp), lambda i: (i, 0)),
                  pl.BlockSpec((LANES, LANES), lambda i: (0, 0))],
        out_specs=pl.BlockSpec((rows, lp), lambda i: (i, 0)),
        out_shape=jax.ShapeDtypeStruct((n, lp), F32),
        compiler_params=pltpu.CompilerParams(dimension_semantics=("arbitrary",),
                                             vmem_limit_bytes=VMEM_LIMIT),
        name="sample_select",
    )(keys, tri)


def _sample_attn_kernel(pt_ref, aq_ref, az_ref, bias_ref, knew_ref, vnew_ref, *rest,
                        n_pages, page, ts):
    del pt_ref
    k_refs, v_refs = rest[:n_pages], rest[n_pages:2 * n_pages]
    ga_ref, s_scr = rest[2 * n_pages], rest[2 * n_pages + 1]
    rows = ATT_HEADS * ts
    aq = aq_ref[...].astype(F32) * (ATT_HEAD_DIM ** -0.5)
    lo_heads, hi_heads = [], []
    for s in range(4):
        lo, hi = _split_heads(aq[:, s * LANES:(s + 1) * LANES], ts)
        lo_heads.append(lo)
        hi_heads.append(hi)
    qs = jnp.concatenate(lo_heads + hi_heads, axis=0).astype(BF16)
    pad = jnp.zeros((page - ts, LANES), F32)
    k_new = jnp.concatenate([knew_ref[...], pad], axis=0).astype(BF16)
    v_new = jnp.concatenate([vnew_ref[...], pad], axis=0).astype(BF16)

    def qk(c):
        if c < n_pages:
            return jnp.dot(qs, k_refs[c][...].astype(BF16), preferred_element_type=F32)
        return lax.dot_general(qs, k_new, NT_DIMS, preferred_element_type=F32)

    def pv(p, c):
        if c < n_pages:
            return lax.dot_general(p, v_refs[c][...].astype(BF16), NT_DIMS,
                                   preferred_element_type=F32)
        return jnp.dot(p, v_new, preferred_element_type=F32)

    mx = jnp.full((rows, LANES), NEG_BIG, F32)
    for c in range(n_pages + 1):
        sl = slice(c * page, (c + 1) * page)
        s = qk(c) + jnp.concatenate([bias_ref[:, sl]] * ATT_HEADS, axis=0)
        s_scr[:, sl] = s
        mx = jnp.maximum(mx, s)
    m = jnp.broadcast_to(jnp.max(mx, axis=1, keepdims=True), (rows, LANES))
    acc = jnp.zeros((rows, LANES), F32)
    psum = jnp.zeros((rows, LANES), F32)
    for c in range(n_pages + 1):
        p = jnp.exp(s_scr[:, c * page:(c + 1) * page] - m)
        psum = psum + p
        acc = acc + pv(p.astype(BF16), c)
    o = acc / jnp.broadcast_to(jnp.sum(psum, axis=1, keepdims=True), (rows, LANES))
    lane = lax.broadcasted_iota(I32, (ts, LANES), 1)
    for s in range(4):
        oo = jnp.where(lane < ATT_HEAD_DIM, o[s * ts:(s + 1) * ts], o[(s + 4) * ts:(s + 5) * ts])
        az = az_ref[:, s * LANES:(s + 1) * LANES].astype(F32)
        ga_ref[:, s * LANES:(s + 1) * LANES] = (oo * _silu(az)).astype(ga_ref.dtype)


def _sample_attn(page_table, aq, az, bias, k_new, v_new, cache_k, cache_v, layer, ts):
    nb, n_pages = page_table.shape
    page = cache_k.shape[3]
    lp = (n_pages + 1) * page
    row = lambda b, pt: (b, 0)
    page_specs = [pl.BlockSpec((None, None, LANES, page),
                               functools.partial(lambda b, pt, p: (layer, pt[b, p], 0, 0), p=p))
                  for p in range(n_pages)]
    grid_spec = pltpu.PrefetchScalarGridSpec(
        num_scalar_prefetch=1,
        grid=(nb,),
        in_specs=[pl.BlockSpec((ts, 512), row),
                  pl.BlockSpec((ts, 512), row),
                  pl.BlockSpec((ts, lp), row),
                  pl.BlockSpec((ts, LANES), row),
                  pl.BlockSpec((ts, LANES), row)] + page_specs + page_specs,
        out_specs=pl.BlockSpec((ts, 512), row),
        scratch_shapes=[pltpu.VMEM((ATT_HEADS * ts, lp), F32)],
    )
    return pl.pallas_call(
        functools.partial(_sample_attn_kernel, n_pages=n_pages, page=page, ts=ts),
        grid_spec=grid_spec,
        out_shape=jax.ShapeDtypeStruct((nb * ts, 512), aq.dtype),
        compiler_params=pltpu.CompilerParams(dimension_semantics=("arbitrary",),
                                             vmem_limit_bytes=VMEM_LIMIT),
        name="sample_attn",
    )(page_table, aq, az, bias, k_new, v_new, *([cache_k] * n_pages), *([cache_v] * n_pages))


def _merge_kernel(x_ref, p_ref, gr_ref, ga_ref, br_ref, ba_ref, wor_ref, woa_ref, wout_ref,
                  wpg_ref, wpp_ref, y_ref):
    u_r = jnp.dot(br_ref[...].astype(BF16), wor_ref[...], preferred_element_type=F32)
    u_a = jnp.dot(ba_ref[...].astype(BF16), woa_ref[...], preferred_element_type=F32)
    m = (jax.nn.sigmoid(gr_ref[...].astype(F32)) * u_r
         + jax.nn.sigmoid(ga_ref[...].astype(F32)) * u_a)
    x1 = x_ref[...] + jnp.dot(m.astype(BF16), wout_ref[...], preferred_element_type=F32)
    gate = jax.nn.sigmoid(jnp.dot(x1.astype(BF16), wpg_ref[...], preferred_element_type=F32))
    y_ref[...] = x1 + gate * jnp.dot(p_ref[...].astype(BF16), wpp_ref[...],
                                     preferred_element_type=F32)


def _merge(x, p, gr, ga, br, ba, wor, woa, wout, wpg, wpp, tm):
    n, d = x.shape
    row = lambda i: (i, 0)
    const = lambda i: (0, 0)
    return pl.pallas_call(
        _merge_kernel,
        grid=(n // tm,),
        in_specs=[pl.BlockSpec((tm, d), row),
                  pl.BlockSpec((tm, p.shape[1]), row),
                  pl.BlockSpec((tm, d), row),
                  pl.BlockSpec((tm, d), row),
                  pl.BlockSpec((tm, 512), row),
                  pl.BlockSpec((tm, 512), row),
                  pl.BlockSpec(wor.shape, const),
                  pl.BlockSpec(woa.shape, const),
                  pl.BlockSpec(wout.shape, const),
                  pl.BlockSpec(wpg.shape, const),
                  pl.BlockSpec(wpp.shape, const)],
        out_specs=pl.BlockSpec((tm, d), row),
        out_shape=jax.ShapeDtypeStruct((n, d), F32),
        compiler_params=pltpu.CompilerParams(dimension_semantics=("arbitrary",),
                                             vmem_limit_bytes=VMEM_LIMIT),
        name="merge",
    )(x, p, gr, ga, br, ba, wor, woa, wout, wpg, wpp)


def _pair_heads_rows(m):
    d = m.shape[1]
    m = m.reshape(ATT_KV_HEADS, ATT_HEADS // ATT_KV_HEADS, ATT_HEAD_DIM, d)
    return jnp.concatenate([m[0], m[1]], axis=1).reshape(ATT_HEADS * ATT_HEAD_DIM, d)


def _pack_w_in(wt):
    sizes = (512, 512, 512, 512, 512, 128, 128, 512, 256, 64, 4, 1024, 1024)
    offs, o = [], 0
    for s in sizes:
        offs.append((o, o + s))
        o += s
    rq, rk, rv, rz, aq, ak, av, az, iq, ik, iw, gr, ga = [wt[a:b] for a, b in offs]
    iwp = jnp.pad(iw, ((0, LANES - IDX_HEADS), (0, 0)))
    packed = jnp.concatenate([rq, rk, rv, rz, _pair_heads_rows(aq), _pair_heads_rows(az), ak, av,
                              iq, ik, ik, iwp, gr, ga], axis=0)
    return packed.astype(BF16)


def _rope_tables(pos):
    pf = pos.astype(F32)[:, None]
    half = RET_DK // 2
    freqs = jnp.exp(-math.log(RET_THETA) * jnp.arange(half, dtype=F32) / half)
    ang = pf * freqs[None, :]
    cos, sin = jnp.cos(ang), jnp.sin(ang)
    rc = jnp.concatenate([cos, cos], axis=1)
    rs = jnp.concatenate([-sin, sin], axis=1)
    half = ROPE_DIM // 2
    freqs = jnp.exp(-math.log(ROPE_THETA) * jnp.arange(half, dtype=F32) / half)
    ang = pf * freqs[None, :]
    cos, sin = jnp.cos(ang), jnp.sin(ang)
    r = pos.shape[0]
    rest = ATT_HEAD_DIM - ROPE_DIM
    one, zero, zh = jnp.ones((r, rest), F32), jnp.zeros((r, rest), F32), jnp.zeros((r, half), F32)
    ac = jnp.concatenate([cos, cos, one], axis=1)
    as1 = jnp.concatenate([-sin, zh, zero], axis=1)
    as2 = jnp.concatenate([zh, sin, zero], axis=1)
    tile2 = lambda a: jnp.concatenate([a, a], axis=1)
    return rc, rs, tile2(ac), tile2(as1), tile2(as2)


def _decay_tables(c_eff):
    h = RET_HEADS
    log_g = jnp.log1p(-jnp.exp2(-5.0 - jnp.arange(h, dtype=F32)))
    c = jnp.arange(RET_CHUNK, dtype=F32)
    diff = c[:, None] - c[None, :]
    di = jnp.where(diff[None] >= 0, jnp.exp(jnp.maximum(diff, 0.0)[None] * log_g[:, None, None]), 0.0)
    dq = jnp.exp((c[:, None] + 1.0) * log_g[None, :])
    dk = jnp.exp((c_eff - 1.0 - c)[:, None] * log_g[None, :])
    dk = jnp.where(c[:, None] < c_eff, dk, 0.0)
    ds = jnp.exp(c_eff * log_g)
    rep = lambda a: jnp.repeat(a, RET_DK, axis=-1)
    return di, rep(dq), rep(dk), rep(ds[None, :])


def _tri(n):
    r = lax.broadcasted_iota(I32, (n, n), 0)
    c = lax.broadcasted_iota(I32, (n, n), 1)
    return jnp.where(r <= c, 1.0, 0.0).astype(BF16)


def _group_sum_matrix():
    r = lax.broadcasted_iota(I32, (LANES, LANES), 0) // ATT_HEAD_DIM
    c = lax.broadcasted_iota(I32, (LANES, LANES), 1) // ATT_HEAD_DIM
    return jnp.where(r == c, 1.0, 0.0).astype(BF16)


def _layer_weights(i, norm_gain, w_in_t, q_norm_gain, k_norm_gain, w_o_ret, w_o_att, w_out,
                   w_ple_gate, w_ple_proj):
    woa = _pair_heads_rows(w_o_att[i])
    return dict(
        gain=norm_gain[i][None, :],
        w_in=_pack_w_in(w_in_t[:, i, :]),
        qg=jnp.tile(q_norm_gain[i], 2)[None, :],
        kg=jnp.tile(k_norm_gain[i], 2)[None, :],
        wor=w_o_ret[i].astype(BF16),
        woa=woa.astype(BF16),
        wout=w_out[i].astype(BF16),
        wpg=w_ple_gate[i].astype(BF16),
        wpp=w_ple_proj[i].astype(BF16),
    )


def kernel(x_prompt, x_sample, cache_k, cache_v, cache_idx_k, state_ret, page_table, p_prompt,
           p_sample, norm_gain, w_in, q_norm_gain, k_norm_gain, w_o_ret, w_o_att, w_out, w_ple_gate,
           w_ple_proj):
    bp, tp, d = x_prompt.shape
    bs, ts, _ = x_sample.shape
    depth = w_in.shape[0]
    n_pool, page = cache_k.shape[1], cache_k.shape[2]
    n_pages = page_table.shape[1]
    past = n_pages * page
    topk_p = min(TOPK_MAX, tp // 4)
    topk_s = min(TOPK_MAX, (past + ts) // 4)
    assert tp % KEY_CHUNK == 0 and tp % RET_CHUNK == 0 and ts <= page and ts % 8 == 0
    assert topk_p <= KEY_CHUNK

    np_, ns_ = bp * tp, bs * ts
    tm_p = 512 if np_ % 512 == 0 else Q_TILE
    tm_s = 256 if ns_ % 256 == 0 else ns_
    assert tm_s % ts == 0 and tp % tm_p == 0

    gsum = _group_sum_matrix()
    tabs_p = _rope_tables(jnp.arange(tp, dtype=I32))
    tabs_s = _rope_tables(past + (jnp.arange(tm_s, dtype=I32) % ts))
    decay_p = _decay_tables(float(RET_CHUNK))
    decay_s = _decay_tables(float(ts))
    tri_s = _tri(LANES)
    tril_p = _tri(LANES).T
    ck = cache_k.reshape(depth, n_pool, page, LANES).transpose(0, 1, 3, 2)
    cv = cache_v.reshape(depth, n_pool, page, LANES).transpose(0, 1, 3, 2)
    cik = cache_idx_k.transpose(0, 1, 3, 2)
    w_in_t = w_in.transpose(2, 0, 1)
    s0_p = jnp.zeros((bp, RET_HEADS, RET_DK, RET_DV), F32)

    xp = x_prompt.reshape(np_, d)
    xs = x_sample.reshape(ns_, d)
    outs = {k: [] for k in ("kp", "vp", "ikp", "sp", "ks", "vs", "iks", "ss")}

    def pad_rows(a):
        w = a.shape[1]
        return jnp.pad(a.reshape(bs, ts, w), ((0, 0), (0, RET_CHUNK - ts), (0, 0)))

    for i in range(depth):
        lw = _layer_weights(i, norm_gain, w_in_t, q_norm_gain, k_norm_gain, w_o_ret, w_o_att, w_out,
                            w_ple_gate, w_ple_proj)

        (rq, rk, rv, rz, aq, az, akb, avt, iq, ikk, gr, ga, ak, av, ik, iw) = _inproj(
            xp, lw["gain"], lw["w_in"], tabs_p, lw["qg"], lw["kg"], gsum, tm_p, tp // tm_p, BF16)
        seq3 = lambda a: a.reshape(bp, tp, 512)
        b_r, s_new = _retention(seq3(rq), seq3(rk), seq3(rv), seq3(rz), s0_p, decay_p, bp,
                                tp // RET_CHUNK, BF16)
        b_r = b_r.reshape(np_, 512)
        b_a = _attn_prompt(aq, iq, iw, az, akb, avt, ikk, tril_p, bp, tp, topk_p)
        xp = _merge(xp, p_prompt[i].reshape(np_, -1), gr, ga, b_r, b_a, lw["wor"], lw["woa"],
                    lw["wout"], lw["wpg"], lw["wpp"], tm_p)
        outs["kp"].append(ak.reshape(bp, tp, ATT_KV_HEADS, ATT_HEAD_DIM))
        outs["vp"].append(av.reshape(bp, tp, ATT_KV_HEADS, ATT_HEAD_DIM))
        outs["ikp"].append(ik.reshape(bp, tp, IDX_DIM))
        outs["sp"].append(s_new)

        (rq, rk, rv, rz, aq, az, akb, avb, iq, ikk, gr, ga, ak, av, ik, iw) = _inproj(
            xs, lw["gain"], lw["w_in"], tabs_s, lw["qg"], lw["kg"], gsum, tm_s, 1, F32)
        b_r, s_new = _retention(pad_rows(rq), pad_rows(rk), pad_rows(rv), pad_rows(rz),
                                state_ret[i], decay_s, bs, 1, F32)
        b_r = b_r[:, :ts].reshape(ns_, 512)
        keys = _sample_scores(page_table, iq, iw, ik, cik, i, ts)
        bias = _sample_select(keys, tri_s, topk_s, ts, past)
        b_a = _sample_attn(page_table, aq, az, bias, ak, av, ck, cv, i, ts)
        xs = _merge(xs, p_sample[i].reshape(ns_, -1), gr, ga, b_r, b_a, lw["wor"], lw["woa"],
                    lw["wout"], lw["wpg"], lw["wpp"], tm_s)
        outs["ks"].append(ak.reshape(bs, ts, ATT_KV_HEADS, ATT_HEAD_DIM))
        outs["vs"].append(av.reshape(bs, ts, ATT_KV_HEADS, ATT_HEAD_DIM))
        outs["iks"].append(ik.reshape(bs, ts, IDX_DIM))
        outs["ss"].append(s_new)

    st = lambda k: jnp.stack(outs[k])
    return (xp.reshape(bp, tp, d), xs.reshape(bs, ts, d), st("kp"), st("vp"), st("ikp"), st("sp"),
            st("ks"), st("vs"), st("iks"), st("ss"))
```

```python
import functools
import math

import jax
import jax.numpy as jnp
from jax import lax
from jax.experimental import pallas as pl
from jax.experimental.pallas import tpu as pltpu

F32 = jnp.float32
BF16 = jnp.bfloat16
I32 = jnp.int32

RET_HEADS = 4
RET_DK = 128
RET_DV = 128
RET_CHUNK = 128
RET_THETA = 10000.0
ATT_HEADS = 8
ATT_KV_HEADS = 2
ATT_HEAD_DIM = 64
ROPE_THETA = 500000.0
ROPE_DIM = ATT_HEAD_DIM // 4
IDX_HEADS = 4
IDX_DIM = 64
TOPK_MAX = 256
NORM_EPS = 1e-6
GN_EPS = 1e-5

LANES = 128
Q_TILE = 128
KEY_CHUNK = 512
VMEM_LIMIT = 56 * 1024 * 1024
NEG_BIG = -1e30
INT_MIN = -(2 ** 31)

C_RQ, C_RK, C_RV, C_RZ = 0, 512, 1024, 1536
C_AQ, C_AZ, C_AK, C_AV = 2048, 2560, 3072, 3200
C_IQ, C_IKK, C_IW, C_GR, C_GA = 3328, 3584, 3712, 3840, 4864
W_PACKED = 5888

NT_DIMS = (((1,), (1,)), ((), ()))
VT_ROWS = LANES + 16
LOG2E = math.log2(math.e)


def _silu(x):
    return x * jax.nn.sigmoid(x)


def _inproj_kernel(x_ref, g_ref, w_ref, rc_ref, rs_ref, ac_ref, as1_ref, as2_ref, qg_ref, kg_ref,
                   gsum_ref,
                   rq_o, rk_o, rv_o, rz_o, aq_o, az_o, akb_o, avt_o, iq_o, ikk_o, gr_o, ga_o,
                   ak_o, av_o, ik_o, iw_o):
    sd = rq_o.dtype
    x = x_ref[...]
    ms = jnp.mean(x * x, axis=-1, keepdims=True)
    hb = ((x * lax.rsqrt(ms + NORM_EPS)) * g_ref[...]).astype(BF16)

    def mm(c0, width):
        return lax.dot_general(hb, w_ref[c0:c0 + width, :], NT_DIMS, preferred_element_type=F32)

    rc, rs = rc_ref[...], rs_ref[...]
    ac, as1, as2 = ac_ref[...], as1_ref[...], as2_ref[...]
    gsum = gsum_ref[...]

    def rope_ret(z):
        return z * rc + pltpu.roll(z, 64, 1) * rs

    def rope_att(z):
        return z * ac + pltpu.roll(z, LANES - 8, 1) * as1 + pltpu.roll(z, 8, 1) * as2

    def head_norm(z, gain):
        sq = z * z
        hi = sq.astype(BF16)
        lo = (sq - hi.astype(F32)).astype(BF16)
        ssq = (jnp.dot(hi, gsum, preferred_element_type=F32)
               + jnp.dot(lo, gsum, preferred_element_type=F32))
        return (z * lax.rsqrt(ssq * (1.0 / ATT_HEAD_DIM) + NORM_EPS)) * gain

    def slab(z, s):
        return z[:, s * LANES:(s + 1) * LANES]

    z = mm(C_RQ, 512)
    for s in range(4):
        rq_o[:, s * LANES:(s + 1) * LANES] = rope_ret(slab(z, s)).astype(sd)
    z = mm(C_RK, 512)
    for s in range(4):
        rk_o[:, s * LANES:(s + 1) * LANES] = (rope_ret(slab(z, s)) * (RET_DK ** -0.5)).astype(sd)
    rv_o[...] = mm(C_RV, 512).astype(sd)
    rz_o[...] = mm(C_RZ, 512).astype(sd)

    qg, kg = qg_ref[...], kg_ref[...]
    z = mm(C_AQ, 512)
    for s in range(4):
        aq_o[:, s * LANES:(s + 1) * LANES] = rope_att(head_norm(slab(z, s), qg)).astype(sd)
    az_o[...] = mm(C_AZ, 512).astype(sd)

    z = mm(C_AK, 256)
    k = rope_att(head_norm(slab(z, 0), kg))
    ak_o[...] = k
    akb_o[...] = k.astype(BF16)
    v = slab(z, 1)
    av_o[...] = v
    avt_o[...] = jnp.concatenate([v.T, jnp.ones((VT_ROWS - LANES, v.shape[0]), F32)],
                                 axis=0).astype(BF16)

    z = mm(C_IQ, 512)
    for s in range(2):
        iq_o[:, s * LANES:(s + 1) * LANES] = rope_att(slab(z, s)).astype(sd)
    ikk = rope_att(slab(z, 2))
    ikk_o[...] = ikk.astype(BF16)
    ik_o[...] = ikk[:, :IDX_DIM]
    iw_o[...] = slab(z, 3)

    gr_o[...] = mm(C_GR, 1024).astype(sd)
    ga_o[...] = mm(C_GA, 1024).astype(sd)


def _inproj(x, gain, w_packed, tabs, qg, kg, gsum, tm, pos_period_tiles, sd):
    n, d = x.shape
    grid = (n // tm,)
    row = lambda i: (i, 0)
    const = lambda i: (0, 0)
    tab = lambda i: (i % pos_period_tiles, 0)
    in_specs = [
        pl.BlockSpec((tm, d), row),
        pl.BlockSpec((1, d), const),
        pl.BlockSpec((W_PACKED, d), const),
    ] + [pl.BlockSpec((tm, LANES), tab)] * 5 + [
        pl.BlockSpec((1, LANES), const),
        pl.BlockSpec((1, LANES), const),
        pl.BlockSpec((LANES, LANES), const),
    ]
    widths = [(512, sd)] * 6 + [(128, BF16), (128, BF16), (256, sd), (128, BF16), (1024, sd), (1024, sd),
                                (128, F32), (128, F32), (IDX_DIM, F32), (128, F32)]
    out_shape = [jax.ShapeDtypeStruct((n, w), dt) for w, dt in widths]
    out_specs = [pl.BlockSpec((tm, w), row) for w, _ in widths]
    out_shape[7] = jax.ShapeDtypeStruct((VT_ROWS, n), BF16)
    out_specs[7] = pl.BlockSpec((VT_ROWS, tm), lambda i: (0, i))
    return pl.pallas_call(
        _inproj_kernel,
        grid=grid,
        in_specs=in_specs,
        out_specs=out_specs,
        out_shape=out_shape,
        compiler_params=pltpu.CompilerParams(dimension_semantics=("arbitrary",),
                                             vmem_limit_bytes=VMEM_LIMIT),
        name="inproj",
    )(x, gain, w_packed, *tabs, qg, kg, gsum)


def _retention_kernel(q_ref, k_ref, v_ref, z_ref, s0_ref, di_ref, dq_ref, dk_ref, ds_ref,
                      o_ref, sout_ref, s_scr, *, group):
    c = pl.program_id(1)
    nc = pl.num_programs(1)

    @pl.when(c == 0)
    def _():
        s_scr[...] = s0_ref[...]

    rows = q_ref.shape[1]

    def load(ref, g, hs):
        x = ref[g, :, hs].astype(F32)
        if rows < RET_CHUNK:
            x = jnp.concatenate([x, jnp.zeros((RET_CHUNK - rows, LANES), F32)], axis=0)
        return x

    for g in range(group):
        for h in range(RET_HEADS):
            hs = slice(h * LANES, (h + 1) * LANES)
            q = load(q_ref, g, hs)
            k = load(k_ref, g, hs)
            v = load(v_ref, g, hs).astype(BF16)
            s_old = s_scr[g, h]
            inner = lax.dot_general(q.astype(BF16), k.astype(BF16), NT_DIMS,
                                    preferred_element_type=F32) * di_ref[h]
            o = (jnp.dot(inner.astype(BF16), v, preferred_element_type=F32)
                 + jnp.dot((q * dq_ref[:, hs]).astype(BF16), s_old.astype(BF16),
                           preferred_element_type=F32))
            kd = (k * dk_ref[:, hs]).T.astype(BF16)
            s_scr[g, h] = s_old * ds_ref[:, hs] + jnp.dot(kd, v, preferred_element_type=F32)
            mu = jnp.mean(o, axis=-1, keepdims=True)
            cen = o - mu
            var = jnp.mean(cen * cen, axis=-1, keepdims=True)
            gn = cen * lax.rsqrt(var + GN_EPS)
            o_ref[g, :, hs] = (gn[:rows] * _silu(z_ref[g, :, hs].astype(F32))).astype(o_ref.dtype)

    @pl.when(c == nc - 1)
    def _():
        sout_ref[...] = s_scr[...]


def _retention(rq, rk, rv, rz, s0_all, layer, decay, nb, nchunks, rows, out_dtype):
    di, dq, dk, ds = decay
    c = RET_CHUNK
    group = max(g for g in (4, 2, 1) if nb % g == 0)
    blk = pl.BlockSpec((group, rows, 512), lambda b, j: (b, j, 0))
    st_in = pl.BlockSpec((None, group, RET_HEADS, RET_DK, RET_DV), lambda b, j: (layer, b, 0, 0, 0))
    st = pl.BlockSpec((group, RET_HEADS, RET_DK, RET_DV), lambda b, j: (b, 0, 0, 0))
    const2 = lambda b, j: (0, 0)
    return pl.pallas_call(
        functools.partial(_retention_kernel, group=group),
        grid=(nb // group, nchunks),
        in_specs=[blk, blk, blk, blk, st_in,
                  pl.BlockSpec((RET_HEADS, c, c), lambda b, j: (0, 0, 0)),
                  pl.BlockSpec((c, 512), const2),
                  pl.BlockSpec((c, 512), const2),
                  pl.BlockSpec((1, 512), const2)],
        out_specs=[blk, st],
        out_shape=[jax.ShapeDtypeStruct((nb, nchunks * rows, 512), out_dtype),
                   jax.ShapeDtypeStruct((nb, RET_HEADS, RET_DK, RET_DV), F32)],
        scratch_shapes=[pltpu.VMEM((group, RET_HEADS, RET_DK, RET_DV), F32)],
        compiler_params=pltpu.CompilerParams(dimension_semantics=("arbitrary", "arbitrary"),
                                             vmem_limit_bytes=VMEM_LIMIT),
        name="retention",
    )(rq, rk, rv, rz, s0_all, di, dq, dk, ds)


KEY_NEG_INF = -2139095041


def _cand_float(key):
    bits = key ^ (lax.shift_right_arithmetic(key, 31) & 0x7FFFFFFF)
    return jnp.where(key < KEY_NEG_INF, -jnp.inf, pltpu.bitcast(bits, F32))


def _split_heads(x2, rows):
    lane = lax.broadcasted_iota(I32, (rows, LANES), 1)
    lo = lane < ATT_HEAD_DIM
    return jnp.where(lo, x2, 0.0), jnp.where(lo, 0.0, x2)


def _indexer_scores(iqs, ikc, iwb, rows, feature_major=False):
    if feature_major:
        lg = jnp.dot(iqs, ikc, preferred_element_type=F32)
    else:
        lg = lax.dot_general(iqs, ikc, NT_DIMS, preferred_element_type=F32)
    sc = None
    for h in range(IDX_HEADS):
        t = jnp.maximum(lg[h * rows:(h + 1) * rows] * (IDX_DIM ** -0.5), 0.0) * iwb[h]
        sc = t if sc is None else sc + t
    return sc


def _count(sc_ref, nchunks, kc, rows, pred):
    def body(c, acc):
        k0 = pl.multiple_of(c * kc, kc)
        kk = sc_ref[:, pl.ds(k0, kc)]
        for j in range(kc // LANES):
            acc = acc + jnp.where(pred(kk[:, j * LANES:(j + 1) * LANES]), 1.0, 0.0)
        return acc
    acc = lax.fori_loop(0, nchunks, body, jnp.zeros((rows, LANES), F32))
    return jnp.broadcast_to(jnp.sum(acc, axis=1, keepdims=True), (rows, LANES))


def _kth_largest(sc_ref, nchunks, kc, rows, topk):
    kf = float(topk)
    c0 = _count(sc_ref, nchunks, kc, rows, lambda kk: kk >= 0.0)
    key = jnp.where(c0 >= kf, 0, INT_MIN).astype(I32)

    def bitstep(b, key):
        cand = key | lax.shift_left(jnp.int32(1), 30 - b)
        cf = _cand_float(cand)
        cnt = _count(sc_ref, nchunks, kc, rows, lambda kk: kk >= cf)
        return jnp.where(cnt >= kf, cand, key)

    tau = _cand_float(lax.fori_loop(0, 31, bitstep, key))
    n_gt = _count(sc_ref, nchunks, kc, rows, lambda kk: kk > tau)
    return tau, kf - n_gt


def _select_bias(kk, tau, need, tie_carry, tri, causal, kc, rows):
    nslab = kc // LANES
    eq = [kk[:, j * LANES:(j + 1) * LANES] == tau for j in range(nslab)]
    eqf = jnp.concatenate([jnp.where(e, 1.0, 0.0) for e in eq], axis=1).astype(BF16)
    rank = jnp.dot(eqf, tri, preferred_element_type=F32)
    total = jnp.dot(eqf, jnp.ones((kc, LANES), BF16), preferred_element_type=F32)
    bias = []
    for j in range(nslab):
        sl = slice(j * LANES, (j + 1) * LANES)
        tie_ok = jnp.where((rank[:, sl] + tie_carry) <= need, 0.0, NEG_BIG)
        b = jnp.where(kk[:, sl] > tau, 0.0, jnp.where(eq[j], tie_ok, NEG_BIG))
        bias.append(jnp.where(causal[:, sl], b, NEG_BIG))
    return bias, tie_carry + total


def _rows8(x):
    return x.reshape(x.shape[0] // 8, 8, LANES)


def _fold8(x8, op):
    return jnp.broadcast_to(op(x8, axis=0, keepdims=True), (8, LANES))


def _reduce0(x3, op, ways=8):
    accs = [x3[j] for j in range(ways)]
    for j in range(ways, x3.shape[0]):
        accs[j % ways] = op(accs[j % ways], x3[j])
    while len(accs) > 1:
        accs = [op(accs[a], accs[a + 1]) for a in range(0, len(accs), 2)]
    return accs[0]


I16 = jnp.int16
I16_MIN = -(2 ** 15)


def _count_f(sc_ref, nchunks, kc, cand, strict, ways=4):
    def body(c, accs):
        k0 = pl.multiple_of(c * kc, kc)
        x = _rows8(sc_ref[pl.ds(k0, kc), :])
        ones = jnp.where((x > cand[None]) if strict else (x >= cand[None]), 1.0, 0.0)
        accs = list(accs)
        for j in range(kc // 8):
            accs[j % ways] = accs[j % ways] + ones[j]
        return tuple(accs)

    accs = lax.fori_loop(0, nchunks, body, tuple(jnp.zeros((8, LANES), F32) for _ in range(ways)))
    return _fold8((accs[0] + accs[1]) + (accs[2] + accs[3]), jnp.sum)


def _count_b(sb_ref, nchunks, kc, cand, ways=4):
    c16 = jnp.concatenate([cand, cand], axis=0).astype(BF16)[None]

    def body(c, accs):
        k0 = pl.multiple_of(c * kc, kc)
        x = sb_ref[pl.ds(k0, kc), :].reshape(kc // 16, 16, LANES)
        ones = jnp.where(x >= c16, jnp.int16(1), jnp.int16(0))
        accs = list(accs)
        for j in range(kc // 16):
            accs[j % ways] = accs[j % ways] + ones[j]
        return tuple(accs)

    accs = lax.fori_loop(0, nchunks, body, tuple(jnp.zeros((16, LANES), I16) for _ in range(ways)))
    tot = ((accs[0] + accs[1]) + (accs[2] + accs[3])).astype(I32).astype(F32)
    return jnp.broadcast_to(jnp.sum(tot, axis=0, keepdims=True), (8, LANES))


def _kth_largest_t(sc_ref, sb_ref, nchunks, kc, topk):
    kf = float(topk)
    zero = jnp.zeros((8, LANES), I32)
    c0 = _count_b(sb_ref, nchunks, kc, _cand_float(zero))
    h = jnp.where(c0 >= kf, 0, I16_MIN).astype(I32)

    def hi_step(b, h):
        cand = h | lax.shift_left(jnp.int32(1), 14 - b)
        cnt = _count_b(sb_ref, nchunks, kc, _cand_float(cand * 65536))
        return jnp.where(cnt >= kf, cand, h)

    h = lax.fori_loop(0, 15, hi_step, h)
    base = jnp.maximum(h, I16_MIN + 1) * 65536 - 32768

    def lo_step(b, o):
        cand = o | lax.shift_left(jnp.int32(1), 16 - b)
        cnt = _count_f(sc_ref, nchunks, kc, _cand_float(base + cand), False)
        return jnp.where(cnt >= kf, cand, o)

    tau = _cand_float(base + lax.fori_loop(0, 17, lo_step, zero))
    return tau, kf - _count_f(sc_ref, nchunks, kc, tau, True)


def _attn_prompt_kernel(aq_ref, iq_ref, iw_ref, az_ref, kb_ref, vt_ref, ik_ref, tril_ref, ga_ref,
                        sc_ref, sb_ref, m_ref, acc_ref, *, topk):
    tq, kc = Q_TILE, KEY_CHUNK
    i = pl.program_id(1)
    nkc = lax.div(i * tq + tq + kc - 1, kc)
    qpos = i * tq + lax.broadcasted_iota(I32, (kc, tq), 1)
    krow = lax.broadcasted_iota(I32, (kc, tq), 0)

    iq = iq_ref[...].astype(F32) * (IDX_DIM ** -0.5)
    parts = []
    for s in range(2):
        parts += list(_split_heads(iq[:, s * LANES:(s + 1) * LANES], tq))
    iqs = jnp.concatenate(parts, axis=0).astype(BF16)
    iwt = (iw_ref[...] * (IDX_HEADS ** -0.5)).T

    def scores(c, carry):
        k0 = pl.multiple_of(c * kc, kc)
        ikc = ik_ref[pl.ds(k0, kc), :]
        sc = None
        for pr in range(IDX_HEADS // 2):
            lg = lax.dot_general(ikc, iqs[pr * 2 * tq:(pr + 1) * 2 * tq], NT_DIMS,
                                 preferred_element_type=F32)
            for hh in range(2):
                h = 2 * pr + hh
                t = jnp.maximum(lg[:, hh * tq:(hh + 1) * tq], 0.0) * iwt[h:h + 1, :]
                sc = t if sc is None else sc + t
        sc = jnp.where(k0 + krow <= qpos, sc, -jnp.inf)
        sc_ref[pl.ds(k0, kc), :] = sc
        sb_ref[pl.ds(k0, kc), :] = sc.astype(BF16)
        return carry

    lax.fori_loop(0, nkc, scores, 0)
    tau, need = _kth_largest_t(sc_ref, sb_ref, nkc, kc, topk)

    aq = aq_ref[...].astype(F32) * (ATT_HEAD_DIM ** -0.5 * LOG2E)
    lo_heads, hi_heads = [], []
    for s in range(4):
        lo, hi = _split_heads(aq[:, s * LANES:(s + 1) * LANES], tq)
        lo_heads.append(lo)
        hi_heads.append(hi)
    qs = jnp.concatenate(lo_heads + hi_heads, axis=0).astype(BF16)

    m_ref[...] = jnp.full(m_ref.shape, NEG_BIG, F32)
    acc_ref[...] = jnp.zeros(acc_ref.shape, F32)
    tril = tril_ref[...]
    nblk = kc // LANES
    need = jnp.where(tau == -jnp.inf, 0.0, need)

    def attend(c, tie_carry):
        k0 = pl.multiple_of(c * kc, kc)
        kk3 = _rows8(sc_ref[pl.ds(k0, kc), :])
        eqf = jnp.where(kk3 == tau[None], 1.0, 0.0).reshape(kc, LANES).astype(BF16)
        ranks = []
        for blk in range(nblk):
            r = jnp.dot(tril, eqf[blk * LANES:(blk + 1) * LANES], preferred_element_type=F32)
            ranks.append(_rows8(r) + tie_carry[None])
            tie_carry = tie_carry + jnp.broadcast_to(r[LANES - 1:LANES, :], (8, LANES))
        rank = jnp.concatenate(ranks, axis=0)
        tie_ok = jnp.where(rank <= need[None], 0.0, NEG_BIG)
        bias = jnp.where(kk3 > tau[None], 0.0, jnp.where(kk3 == tau[None], tie_ok, NEG_BIG))

        kcb = kb_ref[pl.ds(k0, kc), :]
        p_cols, alphas = [], []
        for pr in range(ATT_HEADS // 2):
            s = lax.dot_general(kcb, qs[pr * 2 * tq:(pr + 1) * 2 * tq], NT_DIMS,
                                preferred_element_type=F32)
            for hh in range(2):
                h = 2 * pr + hh
                sh = _rows8(s[:, hh * tq:(hh + 1) * tq]) + bias
                m_old = m_ref[h]
                m_new = jnp.maximum(m_old, _fold8(_reduce0(sh, jnp.maximum), jnp.max))
                alpha = jnp.exp2(m_old - m_new)
                p = jnp.exp2(sh - m_new[None])
                m_ref[h] = m_new
                p_cols.append(p.reshape(kc, tq).astype(BF16))
                alphas.append(alpha[0:1, :])
        pv = jnp.dot(vt_ref[:, pl.ds(k0, kc)], jnp.concatenate(p_cols, axis=1),
                     preferred_element_type=F32)
        acc_ref[...] = jnp.concatenate(alphas, axis=1) * acc_ref[...] + pv
        return tie_carry

    lax.fori_loop(0, nkc, attend, jnp.zeros((8, LANES), F32))

    frow = lax.broadcasted_iota(I32, (LANES, tq), 0)

    def head_out(h):
        cols = slice(h * tq, (h + 1) * tq)
        return acc_ref[0:LANES, cols] / acc_ref[LANES:LANES + 1, cols]

    for s in range(4):
        o = jnp.where(frow < ATT_HEAD_DIM, head_out(s), head_out(s + 4)).T
        az = az_ref[:, s * LANES:(s + 1) * LANES].astype(F32)
        ga_ref[:, s * LANES:(s + 1) * LANES] = (o * _silu(az)).astype(ga_ref.dtype)


def _attn_prompt(aq, iq, iw, az, akb, avt, ikk, tril, nb, t, topk):
    n = aq.shape[0]
    nq = t // Q_TILE
    qrow = lambda b, i: (b * nq + i, 0)
    seq = lambda b, i: (b, 0)
    return pl.pallas_call(
        functools.partial(_attn_prompt_kernel, topk=topk),
        grid=(nb, nq),
        in_specs=[pl.BlockSpec((Q_TILE, 512), qrow),
                  pl.BlockSpec((Q_TILE, 256), qrow),
                  pl.BlockSpec((Q_TILE, LANES), qrow),
                  pl.BlockSpec((Q_TILE, 512), qrow),
                  pl.BlockSpec((t, LANES), seq),
                  pl.BlockSpec((VT_ROWS, t), lambda b, i: (0, b)),
                  pl.BlockSpec((t, LANES), seq),
                  pl.BlockSpec((LANES, LANES), lambda b, i: (0, 0))],
        out_specs=pl.BlockSpec((Q_TILE, 512), qrow),
        out_shape=jax.ShapeDtypeStruct((n, 512), aq.dtype),
        scratch_shapes=[pltpu.VMEM((t, Q_TILE), F32),
                        pltpu.VMEM((t, Q_TILE), BF16),
                        pltpu.VMEM((ATT_HEADS, 8, Q_TILE), F32),
                        pltpu.VMEM((VT_ROWS, ATT_HEADS * Q_TILE), F32)],
        compiler_params=pltpu.CompilerParams(dimension_semantics=("arbitrary", "arbitrary"),
                                             vmem_limit_bytes=VMEM_LIMIT),
        name="attn_prompt",
    )(aq, iq, iw, az, akb, avt, ikk, tril)


def _sample_scores_kernel(pt_ref, iq_ref, iw_ref, iknew_ref, *rest, n_pages, page, ts):
    del pt_ref
    page_refs, sc_ref = rest[:n_pages], rest[n_pages]
    past = n_pages * page
    iq = iq_ref[...].astype(F32)
    heads = [iq[:, h * IDX_DIM:(h + 1) * IDX_DIM] for h in range(IDX_HEADS)]
    iqs = jnp.concatenate(heads, axis=0).astype(BF16)
    iw = iw_ref[...] * (IDX_HEADS ** -0.5)
    iwb_past = [jnp.broadcast_to(iw[:, h:h + 1], (ts, past)) for h in range(IDX_HEADS)]
    iwb = [w[:, :page] for w in iwb_past]
    ikt = jnp.concatenate([r[...].astype(BF16) for r in page_refs], axis=1)
    sc_ref[:, :past] = _indexer_scores(iqs, ikt, iwb_past, ts, feature_major=True)
    new = jnp.concatenate([iknew_ref[...], jnp.zeros((page - ts, IDX_DIM), F32)], axis=0)
    sc = _indexer_scores(iqs, new.astype(BF16), iwb, ts)
    row = lax.broadcasted_iota(I32, (ts, page), 0)
    col = lax.broadcasted_iota(I32, (ts, page), 1)
    sc_ref[:, past:past + page] = jnp.where(col <= row, sc, -jnp.inf)


def _sample_scores(page_table, iq, iw, ik_new, cache_idx, layer, ts):
    nb, n_pages = page_table.shape
    page = cache_idx.shape[3]
    lp = (n_pages + 1) * page
    row = lambda b, pt: (b, 0)
    page_specs = [pl.BlockSpec((None, None, IDX_DIM, page),
                               functools.partial(lambda b, pt, p: (layer, pt[b, p], 0, 0), p=p))
                  for p in range(n_pages)]
    grid_spec = pltpu.PrefetchScalarGridSpec(
        num_scalar_prefetch=1,
        grid=(nb,),
        in_specs=[pl.BlockSpec((ts, 256), row),
                  pl.BlockSpec((ts, LANES), row),
                  pl.BlockSpec((ts, IDX_DIM), row)] + page_specs,
        out_specs=pl.BlockSpec((ts, lp), row),
    )
    return pl.pallas_call(
        functools.partial(_sample_scores_kernel, n_pages=n_pages, page=page, ts=ts),
        grid_spec=grid_spec,
        out_shape=jax.ShapeDtypeStruct((nb * ts, lp), F32),
        compiler_params=pltpu.CompilerParams(dimension_semantics=("arbitrary",),
                                             vmem_limit_bytes=VMEM_LIMIT),
        name="sample_scores",
    )(page_table, iq, iw, ik_new, *([cache_idx] * n_pages))


def _sample_select_kernel(sc_ref, tri_ref, bias_ref, *, topk, ts, past, lp):
    rows = sc_ref.shape[0]
    nch = lp // LANES
    tau, need = _kth_largest(sc_ref, nch, LANES, rows, topk)
    qpos = past + lax.rem(lax.broadcasted_iota(I32, (rows, LANES), 0), ts)
    col = lax.broadcasted_iota(I32, (rows, LANES), 1)
    tri = tri_ref[...]

    def body(c, tie_carry):
        k0 = pl.multiple_of(c * LANES, LANES)
        kk = sc_ref[:, pl.ds(k0, LANES)]
        bias, tie_carry = _select_bias(kk, tau, need, tie_carry, tri, k0 + col <= qpos, LANES, rows)
        bias_ref[:, pl.ds(k0, LANES)] = bias[0]
        return tie_carry

    lax.fori_loop(0, nch, body, jnp.zeros((rows, LANES), F32))


def _sample_select(keys, tri, topk, ts, past):
    n, lp = keys.shape
    rows = 128
    return pl.pallas_call(
        functools.partial(_sample_select_kernel, topk=topk, ts=ts, past=past, lp=lp),
        grid=(n // rows,),
        in_specs=[pl.BlockSpec((rows, lp), lambda i: (i, 0)),
                  pl.BlockSpec((LANES, LANES), lambda i: (0, 0))],
        out_specs=pl.BlockSpec((rows, lp), lambda i: (i, 0)),
        out_shape=jax.ShapeDtypeStruct((n, lp), F32),
        compiler_params=pltpu.CompilerParams(dimension_semantics=("arbitrary",),
                                             vmem_limit_bytes=VMEM_LIMIT),
        name="sample_select",
    )(keys, tri)


def _sample_attn_kernel(pt_ref, aq_ref, az_ref, bias_ref, knew_ref, vnew_ref, *rest,
                        n_pages, page, ts):
    del pt_ref
    k_refs, v_refs = rest[:n_pages], rest[n_pages:2 * n_pages]
    ga_ref = rest[2 * n_pages]
    aq = aq_ref[...].astype(F32) * (ATT_HEAD_DIM ** -0.5)
    lo_heads, hi_heads = [], []
    for s in range(4):
        lo, hi = _split_heads(aq[:, s * LANES:(s + 1) * LANES], ts)
        lo_heads.append(lo)
        hi_heads.append(hi)
    qs = jnp.concatenate(lo_heads + hi_heads, axis=0).astype(BF16)
    pad = jnp.zeros((page - ts, LANES), F32)
    k_new = jnp.concatenate([knew_ref[...], pad], axis=0).astype(BF16)
    v_new = jnp.concatenate([vnew_ref[...], pad], axis=0).astype(BF16)

    past = n_pages * page
    kt = jnp.concatenate([r[...].astype(BF16) for r in k_refs], axis=1)
    vt = jnp.concatenate([r[...].astype(BF16) for r in v_refs], axis=1)
    s = jnp.concatenate([jnp.dot(qs, kt, preferred_element_type=F32),
                         lax.dot_general(qs, k_new, NT_DIMS, preferred_element_type=F32)], axis=1)
    s = s + jnp.concatenate([bias_ref[...]] * ATT_HEADS, axis=0)
    p = jnp.exp(s - jnp.max(s, axis=1, keepdims=True))
    denom = jnp.sum(p, axis=1, keepdims=True)
    pb = p.astype(BF16)
    acc = (lax.dot_general(pb[:, :past], vt, NT_DIMS, preferred_element_type=F32)
           + jnp.dot(pb[:, past:], v_new, preferred_element_type=F32))
    o = acc / denom
    lane = lax.broadcasted_iota(I32, (ts, LANES), 1)
    for s in range(4):
        oo = jnp.where(lane < ATT_HEAD_DIM, o[s * ts:(s + 1) * ts], o[(s + 4) * ts:(s + 5) * ts])
        az = az_ref[:, s * LANES:(s + 1) * LANES].astype(F32)
        ga_ref[:, s * LANES:(s + 1) * LANES] = (oo * _silu(az)).astype(ga_ref.dtype)


def _sample_attn(page_table, aq, az, bias, k_new, v_new, cache_k, cache_v, layer, ts):
    nb, n_pages = page_table.shape
    page = cache_k.shape[3]
    lp = (n_pages + 1) * page
    row = lambda b, pt: (b, 0)
    page_specs = [pl.BlockSpec((None, None, LANES, page),
                               functools.partial(lambda b, pt, p: (layer, pt[b, p], 0, 0), p=p))
                  for p in range(n_pages)]
    grid_spec = pltpu.PrefetchScalarGridSpec(
        num_scalar_prefetch=1,
        grid=(nb,),
        in_specs=[pl.BlockSpec((ts, 512), row),
                  pl.BlockSpec((ts, 512), row),
                  pl.BlockSpec((ts, lp), row),
                  pl.BlockSpec((ts, LANES), row),
                  pl.BlockSpec((ts, LANES), row)] + page_specs + page_specs,
        out_specs=pl.BlockSpec((ts, 512), row),
    )
    return pl.pallas_call(
        functools.partial(_sample_attn_kernel, n_pages=n_pages, page=page, ts=ts),
        grid_spec=grid_spec,
        out_shape=jax.ShapeDtypeStruct((nb * ts, 512), aq.dtype),
        compiler_params=pltpu.CompilerParams(dimension_semantics=("arbitrary",),
                                             vmem_limit_bytes=VMEM_LIMIT),
        name="sample_attn",
    )(page_table, aq, az, bias, k_new, v_new, *([cache_k] * n_pages), *([cache_v] * n_pages))


def _merge_kernel(x_ref, p_ref, gr_ref, ga_ref, br_ref, ba_ref, wor_ref, woa_ref, wout_ref,
                  wpg_ref, wpp_ref, y_ref):
    u_r = jnp.dot(br_ref[...].astype(BF16), wor_ref[...], preferred_element_type=F32)
    u_a = jnp.dot(ba_ref[...].astype(BF16), woa_ref[...], preferred_element_type=F32)
    m = (jax.nn.sigmoid(gr_ref[...].astype(F32)) * u_r
         + jax.nn.sigmoid(ga_ref[...].astype(F32)) * u_a)
    x1 = x_ref[...] + jnp.dot(m.astype(BF16), wout_ref[...], preferred_element_type=F32)
    gate = jax.nn.sigmoid(jnp.dot(x1.astype(BF16), wpg_ref[...], preferred_element_type=F32))
    y_ref[...] = x1 + gate * jnp.dot(p_ref[...].astype(BF16), wpp_ref[...],
                                     preferred_element_type=F32)


def _merge(x, p_all, layer, gr, ga, br, ba, wor, woa, wout, wpg, wpp, tm):
    n, d = x.shape
    row = lambda i: (i, 0)
    const = lambda i: (0, 0)
    return pl.pallas_call(
        _merge_kernel,
        grid=(n // tm,),
        in_specs=[pl.BlockSpec((tm, d), row),
                  pl.BlockSpec((None, tm, p_all.shape[2]), lambda i: (layer, i, 0)),
                  pl.BlockSpec((tm, d), row),
                  pl.BlockSpec((tm, d), row),
                  pl.BlockSpec((tm, 512), row),
                  pl.BlockSpec((tm, 512), row),
                  pl.BlockSpec(wor.shape, const),
                  pl.BlockSpec(woa.shape, const),
                  pl.BlockSpec(wout.shape, const),
                  pl.BlockSpec(wpg.shape, const),
                  pl.BlockSpec(wpp.shape, const)],
        out_specs=pl.BlockSpec((tm, d), row),
        out_shape=jax.ShapeDtypeStruct((n, d), F32),
        compiler_params=pltpu.CompilerParams(dimension_semantics=("arbitrary",),
                                             vmem_limit_bytes=VMEM_LIMIT),
        name="merge",
    )(x, p_all, gr, ga, br, ba, wor, woa, wout, wpg, wpp)


def _pair_heads_rows(m):
    d = m.shape[1]
    m = m.reshape(ATT_KV_HEADS, ATT_HEADS // ATT_KV_HEADS, ATT_HEAD_DIM, d)
    return jnp.concatenate([m[0], m[1]], axis=1).reshape(ATT_HEADS * ATT_HEAD_DIM, d)


def _pack_w_in(wt):
    sizes = (512, 512, 512, 512, 512, 128, 128, 512, 256, 64, 4, 1024, 1024)
    offs, o = [], 0
    for s in sizes:
        offs.append((o, o + s))
        o += s
    rq, rk, rv, rz, aq, ak, av, az, iq, ik, iw, gr, ga = [wt[a:b] for a, b in offs]
    iwp = jnp.pad(iw, ((0, LANES - IDX_HEADS), (0, 0)))
    packed = jnp.concatenate([rq, rk, rv, rz, _pair_heads_rows(aq), _pair_heads_rows(az), ak, av,
                              iq, ik, ik, iwp, gr, ga], axis=0)
    return packed.astype(BF16)


def _rope_tables(pos):
    pf = pos.astype(F32)[:, None]
    half = RET_DK // 2
    freqs = jnp.exp(-math.log(RET_THETA) * jnp.arange(half, dtype=F32) / half)
    ang = pf * freqs[None, :]
    cos, sin = jnp.cos(ang), jnp.sin(ang)
    rc = jnp.concatenate([cos, cos], axis=1)
    rs = jnp.concatenate([-sin, sin], axis=1)
    half = ROPE_DIM // 2
    freqs = jnp.exp(-math.log(ROPE_THETA) * jnp.arange(half, dtype=F32) / half)
    ang = pf * freqs[None, :]
    cos, sin = jnp.cos(ang), jnp.sin(ang)
    r = pos.shape[0]
    rest = ATT_HEAD_DIM - ROPE_DIM
    one, zero, zh = jnp.ones((r, rest), F32), jnp.zeros((r, rest), F32), jnp.zeros((r, half), F32)
    ac = jnp.concatenate([cos, cos, one], axis=1)
    as1 = jnp.concatenate([-sin, zh, zero], axis=1)
    as2 = jnp.concatenate([zh, sin, zero], axis=1)
    tile2 = lambda a: jnp.concatenate([a, a], axis=1)
    return rc, rs, tile2(ac), tile2(as1), tile2(as2)


def _decay_tables(c_eff):
    h = RET_HEADS
    log_g = jnp.log1p(-jnp.exp2(-5.0 - jnp.arange(h, dtype=F32)))
    c = jnp.arange(RET_CHUNK, dtype=F32)
    diff = c[:, None] - c[None, :]
    di = jnp.where(diff[None] >= 0, jnp.exp(jnp.maximum(diff, 0.0)[None] * log_g[:, None, None]), 0.0)
    dq = jnp.exp((c[:, None] + 1.0) * log_g[None, :])
    dk = jnp.exp((c_eff - 1.0 - c)[:, None] * log_g[None, :])
    dk = jnp.where(c[:, None] < c_eff, dk, 0.0)
    ds = jnp.exp(c_eff * log_g)
    rep = lambda a: jnp.repeat(a, RET_DK, axis=-1)
    return di, rep(dq), rep(dk), rep(ds[None, :])


def _tri(n):
    r = lax.broadcasted_iota(I32, (n, n), 0)
    c = lax.broadcasted_iota(I32, (n, n), 1)
    return jnp.where(r <= c, 1.0, 0.0).astype(BF16)


def _group_sum_matrix():
    r = lax.broadcasted_iota(I32, (LANES, LANES), 0) // ATT_HEAD_DIM
    c = lax.broadcasted_iota(I32, (LANES, LANES), 1) // ATT_HEAD_DIM
    return jnp.where(r == c, 1.0, 0.0).astype(BF16)


def _layer_weights(i, norm_gain, w_in_t, q_norm_gain, k_norm_gain, w_o_ret, w_o_att, w_out,
                   w_ple_gate, w_ple_proj):
    woa = _pair_heads_rows(w_o_att[i])
    return dict(
        gain=norm_gain[i][None, :],
        w_in=_pack_w_in(w_in_t[:, i, :]),
        qg=jnp.tile(q_norm_gain[i], 2)[None, :],
        kg=jnp.tile(k_norm_gain[i], 2)[None, :],
        wor=w_o_ret[i].astype(BF16),
        woa=woa.astype(BF16),
        wout=w_out[i].astype(BF16),
        wpg=w_ple_gate[i].astype(BF16),
        wpp=w_ple_proj[i].astype(BF16),
    )


def kernel(x_prompt, x_sample, cache_k, cache_v, cache_idx_k, state_ret, page_table, p_prompt,
           p_sample, norm_gain, w_in, q_norm_gain, k_norm_gain, w_o_ret, w_o_att, w_out, w_ple_gate,
           w_ple_proj):
    bp, tp, d = x_prompt.shape
    bs, ts, _ = x_sample.shape
    depth = w_in.shape[0]
    n_pool, page = cache_k.shape[1], cache_k.shape[2]
    n_pages = page_table.shape[1]
    past = n_pages * page
    topk_p = min(TOPK_MAX, tp // 4)
    topk_s = min(TOPK_MAX, (past + ts) // 4)
    assert tp % KEY_CHUNK == 0 and tp % RET_CHUNK == 0 and ts <= page and ts % 8 == 0
    assert topk_p <= KEY_CHUNK

    np_, ns_ = bp * tp, bs * ts
    tm_p = 512 if np_ % 512 == 0 else Q_TILE
    tm_s = 256 if ns_ % 256 == 0 else ns_
    assert tm_s % ts == 0 and tp % tm_p == 0

    gsum = _group_sum_matrix()
    tabs_p = _rope_tables(jnp.arange(tp, dtype=I32))
    tabs_s = _rope_tables(past + (jnp.arange(tm_s, dtype=I32) % ts))
    decay_p = _decay_tables(float(RET_CHUNK))
    decay_s = _decay_tables(float(ts))
    tri_s = _tri(LANES)
    tril_p = _tri(LANES).T
    ck = cache_k.reshape(depth, n_pool, page, LANES).transpose(0, 1, 3, 2)
    cv = cache_v.reshape(depth, n_pool, page, LANES).transpose(0, 1, 3, 2)
    cik = cache_idx_k.transpose(0, 1, 3, 2)
    w_in_t = w_in.transpose(2, 0, 1)
    s0_p = jnp.zeros((1, bp, RET_HEADS, RET_DK, RET_DV), F32)

    xp = x_prompt.reshape(np_, d)
    xs = x_sample.reshape(ns_, d)
    outs = {k: [] for k in ("kp", "vp", "ikp", "sp", "ks", "vs", "iks", "ss")}

    pp_all = p_prompt.reshape(depth, np_, -1)
    ps_all = p_sample.reshape(depth, ns_, -1)

    for i in range(depth):
        lw = _layer_weights(i, norm_gain, w_in_t, q_norm_gain, k_norm_gain, w_o_ret, w_o_att, w_out,
                            w_ple_gate, w_ple_proj)

        (rq, rk, rv, rz, aq, az, akb, avt, iq, ikk, gr, ga, ak, av, ik, iw) = _inproj(
            xp, lw["gain"], lw["w_in"], tabs_p, lw["qg"], lw["kg"], gsum, tm_p, tp // tm_p, BF16)
        seq3 = lambda a: a.reshape(bp, tp, 512)
        b_r, s_new = _retention(seq3(rq), seq3(rk), seq3(rv), seq3(rz), s0_p, 0, decay_p, bp,
                                tp // RET_CHUNK, RET_CHUNK, BF16)
        b_r = b_r.reshape(np_, 512)
        b_a = _attn_prompt(aq, iq, iw, az, akb, avt, ikk, tril_p, bp, tp, topk_p)
        xp = _merge(xp, pp_all, i, gr, ga, b_r, b_a, lw["wor"], lw["woa"],
                    lw["wout"], lw["wpg"], lw["wpp"], tm_p)
        outs["kp"].append(ak.reshape(bp, tp, ATT_KV_HEADS, ATT_HEAD_DIM))
        outs["vp"].append(av.reshape(bp, tp, ATT_KV_HEADS, ATT_HEAD_DIM))
        outs["ikp"].append(ik.reshape(bp, tp, IDX_DIM))
        outs["sp"].append(s_new)

        (rq, rk, rv, rz, aq, az, akb, avb, iq, ikk, gr, ga, ak, av, ik, iw) = _inproj(
            xs, lw["gain"], lw["w_in"], tabs_s, lw["qg"], lw["kg"], gsum, tm_s, 1, F32)
        tok3 = lambda a: a.reshape(bs, ts, 512)
        b_r, s_new = _retention(tok3(rq), tok3(rk), tok3(rv), tok3(rz), state_ret, i, decay_s, bs,
                                1, ts, F32)
        b_r = b_r.reshape(ns_, 512)
        keys = _sample_scores(page_table, iq, iw, ik, cik, i, ts)
        bias = _sample_select(keys, tri_s, topk_s, ts, past)
        b_a = _sample_attn(page_table, aq, az, bias, ak, av, ck, cv, i, ts)
        xs = _merge(xs, ps_all, i, gr, ga, b_r, b_a, lw["wor"], lw["woa"],
                    lw["wout"], lw["wpg"], lw["wpp"], tm_s)
        outs["ks"].append(ak.reshape(bs, ts, ATT_KV_HEADS, ATT_HEAD_DIM))
        outs["vs"].append(av.reshape(bs, ts, ATT_KV_HEADS, ATT_HEAD_DIM))
        outs["iks"].append(ik.reshape(bs, ts, IDX_DIM))
        outs["ss"].append(s_new)

    st = lambda k: jnp.stack(outs[k])
    return (xp.reshape(bp, tp, d), xs.reshape(bs, ts, d), st("kp"), st("vp"), st("ikp"), st("sp"),
            st("ks"), st("vs"), st("iks"), st("ss"))
```

```python
import functools
import math

import jax
import jax.numpy as jnp
from jax import lax
from jax.experimental import pallas as pl
from jax.experimental.pallas import tpu as pltpu

F32 = jnp.float32
BF16 = jnp.bfloat16
I32 = jnp.int32

RET_HEADS = 4
RET_DK = 128
RET_DV = 128
RET_CHUNK = 128
RET_THETA = 10000.0
ATT_HEADS = 8
ATT_KV_HEADS = 2
ATT_HEAD_DIM = 64
ROPE_THETA = 500000.0
ROPE_DIM = ATT_HEAD_DIM // 4
IDX_HEADS = 4
IDX_DIM = 64
TOPK_MAX = 256
NORM_EPS = 1e-6
GN_EPS = 1e-5

LANES = 128
Q_TILE = 128
KEY_CHUNK = 512
VMEM_LIMIT = 56 * 1024 * 1024
NEG_BIG = -1e30
INT_MIN = -(2 ** 31)

C_RQ, C_RK, C_RV, C_RZ = 0, 512, 1024, 1536
C_AQ, C_AZ, C_AK, C_AV = 2048, 2560, 3072, 3200
C_IQ, C_IKK, C_IW, C_GR, C_GA = 3328, 3584, 3712, 3840, 4864
W_PACKED = 5888

NT_DIMS = (((1,), (1,)), ((), ()))
VT_ROWS = LANES + 16
LOG2E = math.log2(math.e)


def _silu(x):
    return x * jax.nn.sigmoid(x)


def _inproj_kernel(x_ref, g_ref, w_ref, rc_ref, rs_ref, ac_ref, as1_ref, as2_ref, qg_ref, kg_ref,
                   gsum_ref,
                   rq_o, rk_o, rv_o, rz_o, aq_o, az_o, akb_o, avt_o, iq_o, ikk_o, gr_o, ga_o,
                   ak_o, av_o, ik_o, iw_o):
    sd = rq_o.dtype
    x = x_ref[...]
    ms = jnp.mean(x * x, axis=-1, keepdims=True)
    hb = ((x * lax.rsqrt(ms + NORM_EPS)) * g_ref[...]).astype(BF16)

    def mm(c0, width):
        return lax.dot_general(hb, w_ref[c0:c0 + width, :], NT_DIMS, preferred_element_type=F32)

    rc, rs = rc_ref[...], rs_ref[...]
    ac, as1, as2 = ac_ref[...], as1_ref[...], as2_ref[...]
    gsum = gsum_ref[...]

    def rope_ret(z):
        return z * rc + pltpu.roll(z, 64, 1) * rs

    def rope_att(z):
        return z * ac + pltpu.roll(z, LANES - 8, 1) * as1 + pltpu.roll(z, 8, 1) * as2

    def head_norm(z, gain):
        sq = z * z
        hi = sq.astype(BF16)
        lo = (sq - hi.astype(F32)).astype(BF16)
        ssq = (jnp.dot(hi, gsum, preferred_element_type=F32)
               + jnp.dot(lo, gsum, preferred_element_type=F32))
        return (z * lax.rsqrt(ssq * (1.0 / ATT_HEAD_DIM) + NORM_EPS)) * gain

    def slab(z, s):
        return z[:, s * LANES:(s + 1) * LANES]

    z = mm(C_RQ, 512)
    for s in range(4):
        rq_o[:, s * LANES:(s + 1) * LANES] = rope_ret(slab(z, s)).astype(sd)
    z = mm(C_RK, 512)
    for s in range(4):
        rk_o[:, s * LANES:(s + 1) * LANES] = (rope_ret(slab(z, s)) * (RET_DK ** -0.5)).astype(sd)
    rv_o[...] = mm(C_RV, 512).astype(sd)
    rz_o[...] = mm(C_RZ, 512).astype(sd)

    qg, kg = qg_ref[...], kg_ref[...]
    z = mm(C_AQ, 512)
    for s in range(4):
        aq_o[:, s * LANES:(s + 1) * LANES] = rope_att(head_norm(slab(z, s), qg)).astype(sd)
    az_o[...] = mm(C_AZ, 512).astype(sd)

    z = mm(C_AK, 256)
    k = rope_att(head_norm(slab(z, 0), kg))
    ak_o[...] = k
    akb_o[...] = k.astype(BF16)
    v = slab(z, 1)
    av_o[...] = v
    avt_o[...] = jnp.concatenate([v.T, jnp.ones((VT_ROWS - LANES, v.shape[0]), F32)],
                                 axis=0).astype(BF16)

    z = mm(C_IQ, 512)
    for s in range(2):
        iq_o[:, s * LANES:(s + 1) * LANES] = rope_att(slab(z, s)).astype(sd)
    ikk = rope_att(slab(z, 2))
    ikk_o[...] = ikk.astype(BF16)
    ik_o[...] = ikk[:, :IDX_DIM]
    iw_o[...] = slab(z, 3)

    gr_o[...] = mm(C_GR, 1024).astype(sd)
    ga_o[...] = mm(C_GA, 1024).astype(sd)


def _inproj(x, gain, w_packed, tabs, qg, kg, gsum, tm, pos_period_tiles, sd):
    n, d = x.shape
    grid = (n // tm,)
    row = lambda i: (i, 0)
    const = lambda i: (0, 0)
    tab = lambda i: (i % pos_period_tiles, 0)
    in_specs = [
        pl.BlockSpec((tm, d), row),
        pl.BlockSpec((1, d), const),
        pl.BlockSpec((W_PACKED, d), const),
    ] + [pl.BlockSpec((tm, LANES), tab)] * 5 + [
        pl.BlockSpec((1, LANES), const),
        pl.BlockSpec((1, LANES), const),
        pl.BlockSpec((LANES, LANES), const),
    ]
    widths = [(512, sd)] * 6 + [(128, BF16), (128, BF16), (256, sd), (128, BF16), (1024, sd), (1024, sd),
                                (128, F32), (128, F32), (IDX_DIM, F32), (128, F32)]
    out_shape = [jax.ShapeDtypeStruct((n, w), dt) for w, dt in widths]
    out_specs = [pl.BlockSpec((tm, w), row) for w, _ in widths]
    out_shape[7] = jax.ShapeDtypeStruct((VT_ROWS, n), BF16)
    out_specs[7] = pl.BlockSpec((VT_ROWS, tm), lambda i: (0, i))
    return pl.pallas_call(
        _inproj_kernel,
        grid=grid,
        in_specs=in_specs,
        out_specs=out_specs,
        out_shape=out_shape,
        compiler_params=pltpu.CompilerParams(dimension_semantics=("arbitrary",),
                                             vmem_limit_bytes=VMEM_LIMIT),
        name="inproj",
    )(x, gain, w_packed, *tabs, qg, kg, gsum)


def _retention_kernel(q_ref, k_ref, v_ref, z_ref, s0_ref, di_ref, dq_ref, dk_ref, ds_ref,
                      o_ref, sout_ref, s_scr, *, group):
    c = pl.program_id(1)
    nc = pl.num_programs(1)

    @pl.when(c == 0)
    def _():
        s_scr[...] = s0_ref[...]

    rows = q_ref.shape[1]

    def load(ref, g, hs):
        x = ref[g, :, hs].astype(F32)
        if rows < RET_CHUNK:
            x = jnp.concatenate([x, jnp.zeros((RET_CHUNK - rows, LANES), F32)], axis=0)
        return x

    for g in range(group):
        for h in range(RET_HEADS):
            hs = slice(h * LANES, (h + 1) * LANES)
            q = load(q_ref, g, hs)
            k = load(k_ref, g, hs)
            v = load(v_ref, g, hs).astype(BF16)
            s_old = s_scr[g, h]
            inner = lax.dot_general(q.astype(BF16), k.astype(BF16), NT_DIMS,
                                    preferred_element_type=F32) * di_ref[h]
            o = (jnp.dot(inner.astype(BF16), v, preferred_element_type=F32)
                 + jnp.dot((q * dq_ref[:, hs]).astype(BF16), s_old.astype(BF16),
                           preferred_element_type=F32))
            kd = (k * dk_ref[:, hs]).T.astype(BF16)
            s_scr[g, h] = s_old * ds_ref[:, hs] + jnp.dot(kd, v, preferred_element_type=F32)
            mu = jnp.mean(o, axis=-1, keepdims=True)
            cen = o - mu
            var = jnp.mean(cen * cen, axis=-1, keepdims=True)
            gn = cen * lax.rsqrt(var + GN_EPS)
            o_ref[g, :, hs] = (gn[:rows] * _silu(z_ref[g, :, hs].astype(F32))).astype(o_ref.dtype)

    @pl.when(c == nc - 1)
    def _():
        sout_ref[...] = s_scr[...]


def _retention(rq, rk, rv, rz, s0_all, layer, decay, nb, nchunks, rows, out_dtype):
    di, dq, dk, ds = decay
    c = RET_CHUNK
    group = max(g for g in (4, 2, 1) if nb % g == 0)
    blk = pl.BlockSpec((group, rows, 512), lambda b, j: (b, j, 0))
    st_in = pl.BlockSpec((None, group, RET_HEADS, RET_DK, RET_DV), lambda b, j: (layer, b, 0, 0, 0))
    st = pl.BlockSpec((group, RET_HEADS, RET_DK, RET_DV), lambda b, j: (b, 0, 0, 0))
    const2 = lambda b, j: (0, 0)
    return pl.pallas_call(
        functools.partial(_retention_kernel, group=group),
        grid=(nb // group, nchunks),
        in_specs=[blk, blk, blk, blk, st_in,
                  pl.BlockSpec((RET_HEADS, c, c), lambda b, j: (0, 0, 0)),
                  pl.BlockSpec((c, 512), const2),
                  pl.BlockSpec((c, 512), const2),
                  pl.BlockSpec((1, 512), const2)],
        out_specs=[blk, st],
        out_shape=[jax.ShapeDtypeStruct((nb, nchunks * rows, 512), out_dtype),
                   jax.ShapeDtypeStruct((nb, RET_HEADS, RET_DK, RET_DV), F32)],
        scratch_shapes=[pltpu.VMEM((group, RET_HEADS, RET_DK, RET_DV), F32)],
        compiler_params=pltpu.CompilerParams(dimension_semantics=("arbitrary", "arbitrary"),
                                             vmem_limit_bytes=VMEM_LIMIT),
        name="retention",
    )(rq, rk, rv, rz, s0_all, di, dq, dk, ds)


KEY_NEG_INF = -2139095041


def _cand_float(key):
    bits = key ^ (lax.shift_right_arithmetic(key, 31) & 0x7FFFFFFF)
    return jnp.where(key < KEY_NEG_INF, -jnp.inf, pltpu.bitcast(bits, F32))


def _split_heads(x2, rows):
    lane = lax.broadcasted_iota(I32, (rows, LANES), 1)
    lo = lane < ATT_HEAD_DIM
    return jnp.where(lo, x2, 0.0), jnp.where(lo, 0.0, x2)


def _indexer_scores(iqs, ikc, iwb, rows, feature_major=False):
    if feature_major:
        lg = jnp.dot(iqs, ikc, preferred_element_type=F32)
    else:
        lg = lax.dot_general(iqs, ikc, NT_DIMS, preferred_element_type=F32)
    sc = None
    for h in range(IDX_HEADS):
        t = jnp.maximum(lg[h * rows:(h + 1) * rows] * (IDX_DIM ** -0.5), 0.0) * iwb[h]
        sc = t if sc is None else sc + t
    return sc


def _count(sc_ref, nchunks, kc, rows, pred):
    def body(c, acc):
        kk = sc_ref[:, pl.ds(_chunk_start(c, kc), kc)]
        for j in range(kc // LANES):
            acc = acc + jnp.where(pred(kk[:, j * LANES:(j + 1) * LANES]), 1.0, 0.0)
        return acc
    acc = _chunk_loop(nchunks, body, jnp.zeros((rows, LANES), F32))
    return jnp.broadcast_to(jnp.sum(acc, axis=1, keepdims=True), (rows, LANES))


def _kth_largest(sc_ref, nchunks, kc, rows, topk):
    kf = float(topk)
    c0 = _count(sc_ref, nchunks, kc, rows, lambda kk: kk >= 0.0)
    key = jnp.where(c0 >= kf, 0, INT_MIN).astype(I32)

    def bitstep(b, key):
        cand = key | lax.shift_left(jnp.int32(1), 30 - b)
        cf = _cand_float(cand)
        cnt = _count(sc_ref, nchunks, kc, rows, lambda kk: kk >= cf)
        return jnp.where(cnt >= kf, cand, key)

    tau = _cand_float(lax.fori_loop(0, 31, bitstep, key))
    n_gt = _count(sc_ref, nchunks, kc, rows, lambda kk: kk > tau)
    return tau, kf - n_gt


def _select_bias(kk, tau, need, tie_carry, tri, causal, kc, rows):
    nslab = kc // LANES
    eq = [kk[:, j * LANES:(j + 1) * LANES] == tau for j in range(nslab)]
    eqf = jnp.concatenate([jnp.where(e, 1.0, 0.0) for e in eq], axis=1).astype(BF16)
    rank = jnp.dot(eqf, tri, preferred_element_type=F32)
    total = jnp.dot(eqf, jnp.ones((kc, LANES), BF16), preferred_element_type=F32)
    bias = []
    for j in range(nslab):
        sl = slice(j * LANES, (j + 1) * LANES)
        tie_ok = jnp.where((rank[:, sl] + tie_carry) <= need, 0.0, NEG_BIG)
        b = jnp.where(kk[:, sl] > tau, 0.0, jnp.where(eq[j], tie_ok, NEG_BIG))
        bias.append(jnp.where(causal[:, sl], b, NEG_BIG))
    return bias, tie_carry + total


def _rows8(x):
    return x.reshape(x.shape[0] // 8, 8, LANES)


def _fold8(x8, op):
    return jnp.broadcast_to(op(x8, axis=0, keepdims=True), (8, LANES))


def _reduce0(x3, op, ways=8):
    accs = [x3[j] for j in range(ways)]
    for j in range(ways, x3.shape[0]):
        accs[j % ways] = op(accs[j % ways], x3[j])
    while len(accs) > 1:
        accs = [op(accs[a], accs[a + 1]) for a in range(0, len(accs), 2)]
    return accs[0]


I16 = jnp.int16
I16_MIN = -(2 ** 15)


def _chunk_loop(nchunks, body, init):
    if isinstance(nchunks, int):
        for c in range(nchunks):
            init = body(c, init)
        return init
    return lax.fori_loop(0, nchunks, body, init)


def _chunk_start(c, kc):
    return c * kc if isinstance(c, int) else pl.multiple_of(c * kc, kc)


def _count_f(sc_ref, nchunks, kc, cand, strict, ways=4):
    def body(c, accs):
        x = _rows8(sc_ref[pl.ds(_chunk_start(c, kc), kc), :])
        ones = jnp.where((x > cand[None]) if strict else (x >= cand[None]), 1.0, 0.0)
        accs = list(accs)
        for j in range(kc // 8):
            accs[j % ways] = accs[j % ways] + ones[j]
        return tuple(accs)

    accs = _chunk_loop(nchunks, body, tuple(jnp.zeros((8, LANES), F32) for _ in range(ways)))
    return _fold8((accs[0] + accs[1]) + (accs[2] + accs[3]), jnp.sum)


def _count_b(sb_ref, nchunks, kc, cand, ways=4):
    c16 = jnp.concatenate([cand, cand], axis=0).astype(BF16)[None]

    def body(c, accs):
        x = sb_ref[pl.ds(_chunk_start(c, kc), kc), :].reshape(kc // 16, 16, LANES)
        ones = jnp.where(x >= c16, jnp.int16(1), jnp.int16(0))
        accs = list(accs)
        for j in range(kc // 16):
            accs[j % ways] = accs[j % ways] + ones[j]
        return tuple(accs)

    accs = _chunk_loop(nchunks, body, tuple(jnp.zeros((16, LANES), I16) for _ in range(ways)))
    tot = ((accs[0] + accs[1]) + (accs[2] + accs[3])).astype(I32).astype(F32)
    return jnp.broadcast_to(jnp.sum(tot, axis=0, keepdims=True), (8, LANES))


def _kth_largest_t(sc_ref, sb_ref, nchunks, kc, topk):
    kf = float(topk)
    zero = jnp.zeros((8, LANES), I32)
    c0 = _count_b(sb_ref, nchunks, kc, _cand_float(zero))
    h = jnp.where(c0 >= kf, 0, I16_MIN).astype(I32)

    def hi_step(b, h):
        cand = h | lax.shift_left(jnp.int32(1), 14 - b)
        cnt = _count_b(sb_ref, nchunks, kc, _cand_float(cand * 65536))
        return jnp.where(cnt >= kf, cand, h)

    h = lax.fori_loop(0, 15, hi_step, h)
    base = jnp.maximum(h, I16_MIN + 1) * 65536 - 32768

    def lo_step(b, o):
        cand = o | lax.shift_left(jnp.int32(1), 16 - b)
        cnt = _count_f(sc_ref, nchunks, kc, _cand_float(base + cand), False)
        return jnp.where(cnt >= kf, cand, o)

    tau = _cand_float(base + lax.fori_loop(0, 17, lo_step, zero))
    return tau, kf - _count_f(sc_ref, nchunks, kc, tau, True)


def _attn_prompt_kernel(aq_ref, iq_ref, iw_ref, az_ref, kb_ref, vt_ref, ik_ref, tril_ref, ga_ref,
                        sc_ref, sb_ref, m_ref, acc_ref, *, topk):
    tq, kc = Q_TILE, KEY_CHUNK
    i = pl.program_id(1)
    nkc = lax.div(i * tq + tq + kc - 1, kc)
    qpos = i * tq + lax.broadcasted_iota(I32, (kc, tq), 1)
    krow = lax.broadcasted_iota(I32, (kc, tq), 0)

    iq = iq_ref[...].astype(F32) * (IDX_DIM ** -0.5)
    parts = []
    for s in range(2):
        parts += list(_split_heads(iq[:, s * LANES:(s + 1) * LANES], tq))
    iqs = jnp.concatenate(parts, axis=0).astype(BF16)
    iwt = (iw_ref[...] * (IDX_HEADS ** -0.5)).T

    def scores(c, carry):
        k0 = _chunk_start(c, kc)
        ikc = ik_ref[pl.ds(k0, kc), :]
        sc = None
        for pr in range(IDX_HEADS // 2):
            lg = lax.dot_general(ikc, iqs[pr * 2 * tq:(pr + 1) * 2 * tq], NT_DIMS,
                                 preferred_element_type=F32)
            for hh in range(2):
                h = 2 * pr + hh
                t = jnp.maximum(lg[:, hh * tq:(hh + 1) * tq], 0.0) * iwt[h:h + 1, :]
                sc = t if sc is None else sc + t
        sc = jnp.where(k0 + krow <= qpos, sc, -jnp.inf)
        sc_ref[pl.ds(k0, kc), :] = sc
        sb_ref[pl.ds(k0, kc), :] = sc.astype(BF16)
        return carry

    aq = aq_ref[...].astype(F32) * (ATT_HEAD_DIM ** -0.5 * LOG2E)
    lo_heads, hi_heads = [], []
    for s in range(4):
        lo, hi = _split_heads(aq[:, s * LANES:(s + 1) * LANES], tq)
        lo_heads.append(lo)
        hi_heads.append(hi)
    qs = jnp.concatenate(lo_heads + hi_heads, axis=0).astype(BF16)

    m_ref[...] = jnp.full(m_ref.shape, NEG_BIG, F32)
    acc_ref[...] = jnp.zeros(acc_ref.shape, F32)
    tril = tril_ref[...]
    nblk = kc // LANES

    def attend(tau, need, c, tie_carry):
        k0 = _chunk_start(c, kc)
        kk3 = _rows8(sc_ref[pl.ds(k0, kc), :])
        eqf = jnp.where(kk3 == tau[None], 1.0, 0.0).reshape(kc, LANES).astype(BF16)
        ranks = []
        for blk in range(nblk):
            r = jnp.dot(tril, eqf[blk * LANES:(blk + 1) * LANES], preferred_element_type=F32)
            ranks.append(_rows8(r) + tie_carry[None])
            tie_carry = tie_carry + jnp.broadcast_to(r[LANES - 1:LANES, :], (8, LANES))
        rank = jnp.concatenate(ranks, axis=0)
        tie_ok = jnp.where(rank <= need[None], 0.0, NEG_BIG)
        bias = jnp.where(kk3 > tau[None], 0.0, jnp.where(kk3 == tau[None], tie_ok, NEG_BIG))

        kcb = kb_ref[pl.ds(k0, kc), :]
        p_cols, alphas = [], []
        for pr in range(ATT_HEADS // 2):
            s = lax.dot_general(kcb, qs[pr * 2 * tq:(pr + 1) * 2 * tq], NT_DIMS,
                                preferred_element_type=F32)
            for hh in range(2):
                h = 2 * pr + hh
                sh = _rows8(s[:, hh * tq:(hh + 1) * tq]) + bias
                m_old = m_ref[h]
                m_new = jnp.maximum(m_old, _fold8(_reduce0(sh, jnp.maximum), jnp.max))
                alpha = jnp.exp2(m_old - m_new)
                p = jnp.exp2(sh - m_new[None])
                m_ref[h] = m_new
                p_cols.append(p.reshape(kc, tq).astype(BF16))
                alphas.append(alpha[0:1, :])
        pv = jnp.dot(vt_ref[:, pl.ds(k0, kc)], jnp.concatenate(p_cols, axis=1),
                     preferred_element_type=F32)
        acc_ref[...] = jnp.concatenate(alphas, axis=1) * acc_ref[...] + pv
        return tie_carry

    def select(n):
        _chunk_loop(n, scores, 0)
        return _kth_largest_t(sc_ref, sb_ref, n, kc, topk)

    tau, need = lax.switch(nkc - 1, [functools.partial(select, n)
                                     for n in range(1, sc_ref.shape[0] // kc + 1)])
    need = jnp.where(tau == -jnp.inf, 0.0, need)
    sweep = functools.partial(attend, tau, need)

    def sweep2(j, carry):
        return sweep(2 * j + 1, sweep(2 * j, carry))

    carry = lax.fori_loop(0, lax.div(nkc, 2), sweep2, jnp.zeros((8, LANES), F32))

    @pl.when(lax.rem(nkc, 2) == 1)
    def _():
        sweep(nkc - 1, carry)

    frow = lax.broadcasted_iota(I32, (LANES, tq), 0)

    def head_out(h):
        cols = slice(h * tq, (h + 1) * tq)
        return acc_ref[0:LANES, cols] / acc_ref[LANES:LANES + 1, cols]

    for s in range(4):
        o = jnp.where(frow < ATT_HEAD_DIM, head_out(s), head_out(s + 4)).T
        az = az_ref[:, s * LANES:(s + 1) * LANES].astype(F32)
        ga_ref[:, s * LANES:(s + 1) * LANES] = (o * _silu(az)).astype(ga_ref.dtype)


def _attn_prompt(aq, iq, iw, az, akb, avt, ikk, tril, nb, t, topk):
    n = aq.shape[0]
    nq = t // Q_TILE
    qrow = lambda b, i: (b * nq + i, 0)
    seq = lambda b, i: (b, 0)
    return pl.pallas_call(
        functools.partial(_attn_prompt_kernel, topk=topk),
        grid=(nb, nq),
        in_specs=[pl.BlockSpec((Q_TILE, 512), qrow),
                  pl.BlockSpec((Q_TILE, 256), qrow),
                  pl.BlockSpec((Q_TILE, LANES), qrow),
                  pl.BlockSpec((Q_TILE, 512), qrow),
                  pl.BlockSpec((t, LANES), seq),
                  pl.BlockSpec((VT_ROWS, t), lambda b, i: (0, b)),
                  pl.BlockSpec((t, LANES), seq),
                  pl.BlockSpec((LANES, LANES), lambda b, i: (0, 0))],
        out_specs=pl.BlockSpec((Q_TILE, 512), qrow),
        out_shape=jax.ShapeDtypeStruct((n, 512), aq.dtype),
        scratch_shapes=[pltpu.VMEM((t, Q_TILE), F32),
                        pltpu.VMEM((t, Q_TILE), BF16),
                        pltpu.VMEM((ATT_HEADS, 8, Q_TILE), F32),
                        pltpu.VMEM((VT_ROWS, ATT_HEADS * Q_TILE), F32)],
        compiler_params=pltpu.CompilerParams(dimension_semantics=("arbitrary", "arbitrary"),
                                             vmem_limit_bytes=VMEM_LIMIT),
        name="attn_prompt",
    )(aq, iq, iw, az, akb, avt, ikk, tril)


def _sample_scores_kernel(pt_ref, iq_ref, iw_ref, iknew_ref, *rest, n_pages, page, ts):
    del pt_ref
    page_refs, sc_ref = rest[:n_pages], rest[n_pages]
    past = n_pages * page
    iq = iq_ref[...].astype(F32)
    heads = [iq[:, h * IDX_DIM:(h + 1) * IDX_DIM] for h in range(IDX_HEADS)]
    iqs = jnp.concatenate(heads, axis=0).astype(BF16)
    iw = iw_ref[...] * (IDX_HEADS ** -0.5)
    iwb_past = [jnp.broadcast_to(iw[:, h:h + 1], (ts, past)) for h in range(IDX_HEADS)]
    iwb = [w[:, :page] for w in iwb_past]
    ikt = jnp.concatenate([r[...].astype(BF16) for r in page_refs], axis=1)
    sc_ref[:, :past] = _indexer_scores(iqs, ikt, iwb_past, ts, feature_major=True)
    new = jnp.concatenate([iknew_ref[...], jnp.zeros((page - ts, IDX_DIM), F32)], axis=0)
    sc = _indexer_scores(iqs, new.astype(BF16), iwb, ts)
    row = lax.broadcasted_iota(I32, (ts, page), 0)
    col = lax.broadcasted_iota(I32, (ts, page), 1)
    sc_ref[:, past:past + page] = jnp.where(col <= row, sc, -jnp.inf)


def _sample_scores(page_table, iq, iw, ik_new, cache_idx, layer, ts):
    nb, n_pages = page_table.shape
    page = cache_idx.shape[3]
    lp = (n_pages + 1) * page
    row = lambda b, pt: (b, 0)
    page_specs = [pl.BlockSpec((None, None, IDX_DIM, page),
                               functools.partial(lambda b, pt, p: (layer, pt[b, p], 0, 0), p=p))
                  for p in range(n_pages)]
    grid_spec = pltpu.PrefetchScalarGridSpec(
        num_scalar_prefetch=1,
        grid=(nb,),
        in_specs=[pl.BlockSpec((ts, 256), row),
                  pl.BlockSpec((ts, LANES), row),
                  pl.BlockSpec((ts, IDX_DIM), row)] + page_specs,
        out_specs=pl.BlockSpec((ts, lp), row),
    )
    return pl.pallas_call(
        functools.partial(_sample_scores_kernel, n_pages=n_pages, page=page, ts=ts),
        grid_spec=grid_spec,
        out_shape=jax.ShapeDtypeStruct((nb * ts, lp), F32),
        compiler_params=pltpu.CompilerParams(dimension_semantics=("arbitrary",),
                                             vmem_limit_bytes=VMEM_LIMIT),
        name="sample_scores",
    )(page_table, iq, iw, ik_new, *([cache_idx] * n_pages))


def _sample_select_kernel(sc_ref, tri_ref, bias_ref, *, topk, ts, past, lp):
    rows = sc_ref.shape[0]
    nch = lp // LANES
    tau, need = _kth_largest(sc_ref, nch, LANES, rows, topk)
    qpos = past + lax.rem(lax.broadcasted_iota(I32, (rows, LANES), 0), ts)
    col = lax.broadcasted_iota(I32, (rows, LANES), 1)
    tri = tri_ref[...]

    def body(c, tie_carry):
        k0 = pl.multiple_of(c * LANES, LANES)
        kk = sc_ref[:, pl.ds(k0, LANES)]
        bias, tie_carry = _select_bias(kk, tau, need, tie_carry, tri, k0 + col <= qpos, LANES, rows)
        bias_ref[:, pl.ds(k0, LANES)] = bias[0]
        return tie_carry

    lax.fori_loop(0, nch, body, jnp.zeros((rows, LANES), F32))


def _sample_select(keys, tri, topk, ts, past):
    n, lp = keys.shape
    rows = 128
    return pl.pallas_call(
        functools.partial(_sample_select_kernel, topk=topk, ts=ts, past=past, lp=lp),
        grid=(n // rows,),
        in_specs=[pl.BlockSpec((rows, lp), lambda i: (i, 0)),
                  pl.BlockSpec((LANES, LANES), lambda i: (0, 0))],
        out_specs=pl.BlockSpec((rows, lp), lambda i: (i, 0)),
        out_shape=jax.ShapeDtypeStruct((n, lp), F32),
        compiler_params=pltpu.CompilerParams(dimension_semantics=("arbitrary",),
                                             vmem_limit_bytes=VMEM_LIMIT),
        name="sample_select",
    )(keys, tri)


def _sample_attn_kernel(pt_ref, aq_ref, az_ref, bias_ref, knew_ref, vnew_ref, *rest,
                        n_pages, page, ts):
    del pt_ref
    k_refs, v_refs = rest[:n_pages], rest[n_pages:2 * n_pages]
    ga_ref = rest[2 * n_pages]
    aq = aq_ref[...].astype(F32) * (ATT_HEAD_DIM ** -0.5)
    lo_heads, hi_heads = [], []
    for s in range(4):
        lo, hi = _split_heads(aq[:, s * LANES:(s + 1) * LANES], ts)
        lo_heads.append(lo)
        hi_heads.append(hi)
    qs = jnp.concatenate(lo_heads + hi_heads, axis=0).astype(BF16)
    pad = jnp.zeros((page - ts, LANES), F32)
    k_new = jnp.concatenate([knew_ref[...], pad], axis=0).astype(BF16)
    v_new = jnp.concatenate([vnew_ref[...], pad], axis=0).astype(BF16)

    past = n_pages * page
    kt = jnp.concatenate([r[...].astype(BF16) for r in k_refs], axis=1)
    vt = jnp.concatenate([r[...].astype(BF16) for r in v_refs], axis=1)
    s = jnp.concatenate([jnp.dot(qs, kt, preferred_element_type=F32),
                         lax.dot_general(qs, k_new, NT_DIMS, preferred_element_type=F32)], axis=1)
    s = s + jnp.concatenate([bias_ref[...]] * ATT_HEADS, axis=0)
    p = jnp.exp(s - jnp.max(s, axis=1, keepdims=True))
    denom = jnp.sum(p, axis=1, keepdims=True)
    pb = p.astype(BF16)
    acc = (lax.dot_general(pb[:, :past], vt, NT_DIMS, preferred_element_type=F32)
           + jnp.dot(pb[:, past:], v_new, preferred_element_type=F32))
    o = acc / denom
    lane = lax.broadcasted_iota(I32, (ts, LANES), 1)
    for s in range(4):
        oo = jnp.where(lane < ATT_HEAD_DIM, o[s * ts:(s + 1) * ts], o[(s + 4) * ts:(s + 5) * ts])
        az = az_ref[:, s * LANES:(s + 1) * LANES].astype(F32)
        ga_ref[:, s * LANES:(s + 1) * LANES] = (oo * _silu(az)).astype(ga_ref.dtype)


def _sample_attn(page_table, aq, az, bias, k_new, v_new, cache_k, cache_v, layer, ts):
    nb, n_pages = page_table.shape
    page = cache_k.shape[3]
    lp = (n_pages + 1) * page
    row = lambda b, pt: (b, 0)
    page_specs = [pl.BlockSpec((None, None, LANES, page),
                               functools.partial(lambda b, pt, p: (layer, pt[b, p], 0, 0), p=p))
                  for p in range(n_pages)]
    grid_spec = pltpu.PrefetchScalarGridSpec(
        num_scalar_prefetch=1,
        grid=(nb,),
        in_specs=[pl.BlockSpec((ts, 512), row),
                  pl.BlockSpec((ts, 512), row),
                  pl.BlockSpec((ts, lp), row),
                  pl.BlockSpec((ts, LANES), row),
                  pl.BlockSpec((ts, LANES), row)] + page_specs + page_specs,
        out_specs=pl.BlockSpec((ts, 512), row),
    )
    return pl.pallas_call(
        functools.partial(_sample_attn_kernel, n_pages=n_pages, page=page, ts=ts),
        grid_spec=grid_spec,
        out_shape=jax.ShapeDtypeStruct((nb * ts, 512), aq.dtype),
        compiler_params=pltpu.CompilerParams(dimension_semantics=("arbitrary",),
                                             vmem_limit_bytes=VMEM_LIMIT),
        name="sample_attn",
    )(page_table, aq, az, bias, k_new, v_new, *([cache_k] * n_pages), *([cache_v] * n_pages))


def _merge_kernel(x_ref, p_ref, gr_ref, ga_ref, br_ref, ba_ref, wor_ref, woa_ref, wout_ref,
                  wpg_ref, wpp_ref, y_ref):
    u_r = jnp.dot(br_ref[...].astype(BF16), wor_ref[...], preferred_element_type=F32)
    u_a = jnp.dot(ba_ref[...].astype(BF16), woa_ref[...], preferred_element_type=F32)
    m = (jax.nn.sigmoid(gr_ref[...].astype(F32)) * u_r
         + jax.nn.sigmoid(ga_ref[...].astype(F32)) * u_a)
    x1 = x_ref[...] + jnp.dot(m.astype(BF16), wout_ref[...], preferred_element_type=F32)
    gate = jax.nn.sigmoid(jnp.dot(x1.astype(BF16), wpg_ref[...], preferred_element_type=F32))
    y_ref[...] = x1 + gate * jnp.dot(p_ref[...].astype(BF16), wpp_ref[...],
                                     preferred_element_type=F32)


def _merge(x, p_all, layer, gr, ga, br, ba, wor, woa, wout, wpg, wpp, tm):
    n, d = x.shape
    row = lambda i: (i, 0)
    const = lambda i: (0, 0)
    return pl.pallas_call(
        _merge_kernel,
        grid=(n // tm,),
        in_specs=[pl.BlockSpec((tm, d), row),
                  pl.BlockSpec((None, tm, p_all.shape[2]), lambda i: (layer, i, 0)),
                  pl.BlockSpec((tm, d), row),
                  pl.BlockSpec((tm, d), row),
                  pl.BlockSpec((tm, 512), row),
                  pl.BlockSpec((tm, 512), row),
                  pl.BlockSpec(wor.shape, const),
                  pl.BlockSpec(woa.shape, const),
                  pl.BlockSpec(wout.shape, const),
                  pl.BlockSpec(wpg.shape, const),
                  pl.BlockSpec(wpp.shape, const)],
        out_specs=pl.BlockSpec((tm, d), row),
        out_shape=jax.ShapeDtypeStruct((n, d), F32),
        compiler_params=pltpu.CompilerParams(dimension_semantics=("arbitrary",),
                                             vmem_limit_bytes=VMEM_LIMIT),
        name="merge",
    )(x, p_all, gr, ga, br, ba, wor, woa, wout, wpg, wpp)


def _pair_heads_rows(m):
    d = m.shape[1]
    m = m.reshape(ATT_KV_HEADS, ATT_HEADS // ATT_KV_HEADS, ATT_HEAD_DIM, d)
    return jnp.concatenate([m[0], m[1]], axis=1).reshape(ATT_HEADS * ATT_HEAD_DIM, d)


def _pack_w_in(wt):
    sizes = (512, 512, 512, 512, 512, 128, 128, 512, 256, 64, 4, 1024, 1024)
    offs, o = [], 0
    for s in sizes:
        offs.append((o, o + s))
        o += s
    rq, rk, rv, rz, aq, ak, av, az, iq, ik, iw, gr, ga = [wt[a:b] for a, b in offs]
    iwp = jnp.pad(iw, ((0, LANES - IDX_HEADS), (0, 0)))
    packed = jnp.concatenate([rq, rk, rv, rz, _pair_heads_rows(aq), _pair_heads_rows(az), ak, av,
                              iq, ik, ik, iwp, gr, ga], axis=0)
    return packed.astype(BF16)


def _rope_tables(pos):
    pf = pos.astype(F32)[:, None]
    half = RET_DK // 2
    freqs = jnp.exp(-math.log(RET_THETA) * jnp.arange(half, dtype=F32) / half)
    ang = pf * freqs[None, :]
    cos, sin = jnp.cos(ang), jnp.sin(ang)
    rc = jnp.concatenate([cos, cos], axis=1)
    rs = jnp.concatenate([-sin, sin], axis=1)
    half = ROPE_DIM // 2
    freqs = jnp.exp(-math.log(ROPE_THETA) * jnp.arange(half, dtype=F32) / half)
    ang = pf * freqs[None, :]
    cos, sin = jnp.cos(ang), jnp.sin(ang)
    r = pos.shape[0]
    rest = ATT_HEAD_DIM - ROPE_DIM
    one, zero, zh = jnp.ones((r, rest), F32), jnp.zeros((r, rest), F32), jnp.zeros((r, half), F32)
    ac = jnp.concatenate([cos, cos, one], axis=1)
    as1 = jnp.concatenate([-sin, zh, zero], axis=1)
    as2 = jnp.concatenate([zh, sin, zero], axis=1)
    tile2 = lambda a: jnp.concatenate([a, a], axis=1)
    return rc, rs, tile2(ac), tile2(as1), tile2(as2)


def _decay_tables(c_eff):
    h = RET_HEADS
    log_g = jnp.log1p(-jnp.exp2(-5.0 - jnp.arange(h, dtype=F32)))
    c = jnp.arange(RET_CHUNK, dtype=F32)
    diff = c[:, None] - c[None, :]
    di = jnp.where(diff[None] >= 0, jnp.exp(jnp.maximum(diff, 0.0)[None] * log_g[:, None, None]), 0.0)
    dq = jnp.exp((c[:, None] + 1.0) * log_g[None, :])
    dk = jnp.exp((c_eff - 1.0 - c)[:, None] * log_g[None, :])
    dk = jnp.where(c[:, None] < c_eff, dk, 0.0)
    ds = jnp.exp(c_eff * log_g)
    rep = lambda a: jnp.repeat(a, RET_DK, axis=-1)
    return di, rep(dq), rep(dk), rep(ds[None, :])


def _tri(n):
    r = lax.broadcasted_iota(I32, (n, n), 0)
    c = lax.broadcasted_iota(I32, (n, n), 1)
    return jnp.where(r <= c, 1.0, 0.0).astype(BF16)


def _group_sum_matrix():
    r = lax.broadcasted_iota(I32, (LANES, LANES), 0) // ATT_HEAD_DIM
    c = lax.broadcasted_iota(I32, (LANES, LANES), 1) // ATT_HEAD_DIM
    return jnp.where(r == c, 1.0, 0.0).astype(BF16)


def _layer_weights(i, norm_gain, w_in_t, q_norm_gain, k_norm_gain, w_o_ret, w_o_att, w_out,
                   w_ple_gate, w_ple_proj):
    woa = _pair_heads_rows(w_o_att[i])
    return dict(
        gain=norm_gain[i][None, :],
        w_in=_pack_w_in(w_in_t[:, i, :]),
        qg=jnp.tile(q_norm_gain[i], 2)[None, :],
        kg=jnp.tile(k_norm_gain[i], 2)[None, :],
        wor=w_o_ret[i].astype(BF16),
        woa=woa.astype(BF16),
        wout=w_out[i].astype(BF16),
        wpg=w_ple_gate[i].astype(BF16),
        wpp=w_ple_proj[i].astype(BF16),
    )


def kernel(x_prompt, x_sample, cache_k, cache_v, cache_idx_k, state_ret, page_table, p_prompt,
           p_sample, norm_gain, w_in, q_norm_gain, k_norm_gain, w_o_ret, w_o_att, w_out, w_ple_gate,
           w_ple_proj):
    bp, tp, d = x_prompt.shape
    bs, ts, _ = x_sample.shape
    depth = w_in.shape[0]
    n_pool, page = cache_k.shape[1], cache_k.shape[2]
    n_pages = page_table.shape[1]
    past = n_pages * page
    topk_p = min(TOPK_MAX, tp // 4)
    topk_s = min(TOPK_MAX, (past + ts) // 4)
    assert tp % KEY_CHUNK == 0 and tp % RET_CHUNK == 0 and ts <= page and ts % 8 == 0
    assert topk_p <= KEY_CHUNK

    np_, ns_ = bp * tp, bs * ts
    tm_p = 512 if np_ % 512 == 0 else Q_TILE
    tm_s = 256 if ns_ % 256 == 0 else ns_
    assert tm_s % ts == 0 and tp % tm_p == 0

    gsum = _group_sum_matrix()
    tabs_p = _rope_tables(jnp.arange(tp, dtype=I32))
    tabs_s = _rope_tables(past + (jnp.arange(tm_s, dtype=I32) % ts))
    decay_p = _decay_tables(float(RET_CHUNK))
    decay_s = _decay_tables(float(ts))
    tri_s = _tri(LANES)
    tril_p = _tri(LANES).T
    ck = cache_k.reshape(depth, n_pool, page, LANES).transpose(0, 1, 3, 2)
    cv = cache_v.reshape(depth, n_pool, page, LANES).transpose(0, 1, 3, 2)
    cik = cache_idx_k.transpose(0, 1, 3, 2)
    w_in_t = w_in.transpose(2, 0, 1)
    s0_p = jnp.zeros((1, bp, RET_HEADS, RET_DK, RET_DV), F32)

    xp = x_prompt.reshape(np_, d)
    xs = x_sample.reshape(ns_, d)
    outs = {k: [] for k in ("kp", "vp", "ikp", "sp", "ks", "vs", "iks", "ss")}

    pp_all = p_prompt.reshape(depth, np_, -1)
    ps_all = p_sample.reshape(depth, ns_, -1)

    for i in range(depth):
        lw = _layer_weights(i, norm_gain, w_in_t, q_norm_gain, k_norm_gain, w_o_ret, w_o_att, w_out,
                            w_ple_gate, w_ple_proj)

        (rq, rk, rv, rz, aq, az, akb, avt, iq, ikk, gr, ga, ak, av, ik, iw) = _inproj(
            xp, lw["gain"], lw["w_in"], tabs_p, lw["qg"], lw["kg"], gsum, tm_p, tp // tm_p, BF16)
        seq3 = lambda a: a.reshape(bp, tp, 512)
        b_r, s_new = _retention(seq3(rq), seq3(rk), seq3(rv), seq3(rz), s0_p, 0, decay_p, bp,
                                tp // RET_CHUNK, RET_CHUNK, BF16)
        b_r = b_r.reshape(np_, 512)
        b_a = _attn_prompt(aq, iq, iw, az, akb, avt, ikk, tril_p, bp, tp, topk_p)
        xp = _merge(xp, pp_all, i, gr, ga, b_r, b_a, lw["wor"], lw["woa"],
                    lw["wout"], lw["wpg"], lw["wpp"], tm_p)
        outs["kp"].append(ak.reshape(bp, tp, ATT_KV_HEADS, ATT_HEAD_DIM))
        outs["vp"].append(av.reshape(bp, tp, ATT_KV_HEADS, ATT_HEAD_DIM))
        outs["ikp"].append(ik.reshape(bp, tp, IDX_DIM))
        outs["sp"].append(s_new)

        (rq, rk, rv, rz, aq, az, akb, avb, iq, ikk, gr, ga, ak, av, ik, iw) = _inproj(
            xs, lw["gain"], lw["w_in"], tabs_s, lw["qg"], lw["kg"], gsum, tm_s, 1, F32)
        tok3 = lambda a: a.reshape(bs, ts, 512)
        b_r, s_new = _retention(tok3(rq), tok3(rk), tok3(rv), tok3(rz), state_ret, i, decay_s, bs,
                                1, ts, F32)
        b_r = b_r.reshape(ns_, 512)
        keys = _sample_scores(page_table, iq, iw, ik, cik, i, ts)
        bias = _sample_select(keys, tri_s, topk_s, ts, past)
        b_a = _sample_attn(page_table, aq, az, bias, ak, av, ck, cv, i, ts)
        xs = _merge(xs, ps_all, i, gr, ga, b_r, b_a, lw["wor"], lw["woa"],
                    lw["wout"], lw["wpg"], lw["wpp"], tm_s)
        outs["ks"].append(ak.reshape(bs, ts, ATT_KV_HEADS, ATT_HEAD_DIM))
        outs["vs"].append(av.reshape(bs, ts, ATT_KV_HEADS, ATT_HEAD_DIM))
        outs["iks"].append(ik.reshape(bs, ts, IDX_DIM))
        outs["ss"].append(s_new)

    st = lambda k: jnp.stack(outs[k])
    return (xp.reshape(bp, tp, d), xs.reshape(bs, ts, d), st("kp"), st("vp"), st("ikp"), st("sp"),
            st("ks"), st("vs"), st("iks"), st("ss"))
```

```python
import functools
import math

import jax
import jax.numpy as jnp
from jax import lax
from jax.experimental import pallas as pl
from jax.experimental.pallas import tpu as pltpu

F32 = jnp.float32
BF16 = jnp.bfloat16
I32 = jnp.int32

RET_HEADS = 4
RET_DK = 128
RET_DV = 128
RET_CHUNK = 128
RET_THETA = 10000.0
ATT_HEADS = 8
ATT_KV_HEADS = 2
ATT_HEAD_DIM = 64
ROPE_THETA = 500000.0
ROPE_DIM = ATT_HEAD_DIM // 4
IDX_HEADS = 4
IDX_DIM = 64
TOPK_MAX = 256
NORM_EPS = 1e-6
GN_EPS = 1e-5

LANES = 128
Q_TILE = 128
KEY_CHUNK = 512
VMEM_LIMIT = 56 * 1024 * 1024
NEG_BIG = -1e30

C_RQ, C_RK, C_RV, C_RZ = 0, 512, 1024, 1536
C_AQ, C_AZ, C_AK, C_AV = 2048, 2560, 3072, 3200
C_IQ, C_IKK, C_IW, C_GR, C_GA = 3328, 3584, 3712, 3840, 4864
W_PACKED = 5888

NT_DIMS = (((1,), (1,)), ((), ()))
VT_ROWS = LANES + 16
LOG2E = math.log2(math.e)


def _silu(x):
    return x * jax.nn.sigmoid(x)


def _inproj_kernel(x_ref, g_ref, w_ref, rc_ref, rs_ref, ac_ref, as1_ref, as2_ref, qg_ref, kg_ref,
                   gsum_ref,
                   rq_o, rk_o, rv_o, rz_o, aq_o, az_o, akb_o, avt_o, iq_o, ikk_o, gr_o, ga_o,
                   ak_o, av_o, ik_o, iw_o):
    sd = rq_o.dtype
    x = x_ref[...]
    ms = jnp.mean(x * x, axis=-1, keepdims=True)
    hb = ((x * lax.rsqrt(ms + NORM_EPS)) * g_ref[...]).astype(BF16)

    def mm(c0, width):
        return lax.dot_general(hb, w_ref[c0:c0 + width, :], NT_DIMS, preferred_element_type=F32)

    rc, rs = rc_ref[...], rs_ref[...]
    ac, as1, as2 = ac_ref[...], as1_ref[...], as2_ref[...]
    gsum = gsum_ref[...]

    def rope_ret(z):
        return z * rc + pltpu.roll(z, 64, 1) * rs

    def rope_att(z):
        return z * ac + pltpu.roll(z, LANES - 8, 1) * as1 + pltpu.roll(z, 8, 1) * as2

    def head_norm(z, gain):
        sq = z * z
        hi = sq.astype(BF16)
        lo = (sq - hi.astype(F32)).astype(BF16)
        ssq = (jnp.dot(hi, gsum, preferred_element_type=F32)
               + jnp.dot(lo, gsum, preferred_element_type=F32))
        return (z * lax.rsqrt(ssq * (1.0 / ATT_HEAD_DIM) + NORM_EPS)) * gain

    def slab(z, s):
        return z[:, s * LANES:(s + 1) * LANES]

    z = mm(C_RQ, 512)
    for s in range(4):
        rq_o[:, s * LANES:(s + 1) * LANES] = rope_ret(slab(z, s)).astype(sd)
    z = mm(C_RK, 512)
    for s in range(4):
        rk_o[:, s * LANES:(s + 1) * LANES] = (rope_ret(slab(z, s)) * (RET_DK ** -0.5)).astype(sd)
    rv_o[...] = mm(C_RV, 512).astype(sd)
    rz_o[...] = mm(C_RZ, 512).astype(sd)

    qg, kg = qg_ref[...], kg_ref[...]
    z = mm(C_AQ, 512)
    for s in range(4):
        aq_o[:, s * LANES:(s + 1) * LANES] = rope_att(head_norm(slab(z, s), qg)).astype(sd)
    az_o[...] = mm(C_AZ, 512).astype(sd)

    z = mm(C_AK, 256)
    k = rope_att(head_norm(slab(z, 0), kg))
    ak_o[...] = k
    akb_o[...] = k.astype(BF16)
    v = slab(z, 1)
    av_o[...] = v
    avt_o[...] = jnp.concatenate([v.T, jnp.ones((VT_ROWS - LANES, v.shape[0]), F32)],
                                 axis=0).astype(BF16)

    z = mm(C_IQ, 512)
    for s in range(2):
        iq_o[:, s * LANES:(s + 1) * LANES] = rope_att(slab(z, s)).astype(sd)
    ikk = rope_att(slab(z, 2))
    ikk_o[...] = ikk.astype(BF16)
    ik_o[...] = ikk[:, :IDX_DIM]
    iw_o[...] = slab(z, 3)

    gr_o[...] = mm(C_GR, 1024).astype(sd)
    ga_o[...] = mm(C_GA, 1024).astype(sd)


def _inproj(x, gain, w_packed, tabs, qg, kg, gsum, tm, pos_period_tiles, sd):
    n, d = x.shape
    grid = (n // tm,)
    row = lambda i: (i, 0)
    const = lambda i: (0, 0)
    tab = lambda i: (i % pos_period_tiles, 0)
    in_specs = [
        pl.BlockSpec((tm, d), row),
        pl.BlockSpec((1, d), const),
        pl.BlockSpec((W_PACKED, d), const),
    ] + [pl.BlockSpec((tm, LANES), tab)] * 5 + [
        pl.BlockSpec((1, LANES), const),
        pl.BlockSpec((1, LANES), const),
        pl.BlockSpec((LANES, LANES), const),
    ]
    widths = [(512, sd)] * 6 + [(128, BF16), (128, BF16), (256, sd), (128, BF16), (1024, sd), (1024, sd),
                                (128, F32), (128, F32), (IDX_DIM, F32), (128, F32)]
    out_shape = [jax.ShapeDtypeStruct((n, w), dt) for w, dt in widths]
    out_specs = [pl.BlockSpec((tm, w), row) for w, _ in widths]
    out_shape[7] = jax.ShapeDtypeStruct((VT_ROWS, n), BF16)
    out_specs[7] = pl.BlockSpec((VT_ROWS, tm), lambda i: (0, i))
    return pl.pallas_call(
        _inproj_kernel,
        grid=grid,
        in_specs=in_specs,
        out_specs=out_specs,
        out_shape=out_shape,
        compiler_params=pltpu.CompilerParams(dimension_semantics=("arbitrary",),
                                             vmem_limit_bytes=VMEM_LIMIT),
        name="inproj",
    )(x, gain, w_packed, *tabs, qg, kg, gsum)


def _retention_kernel(q_ref, k_ref, v_ref, z_ref, s0_ref, di_ref, dq_ref, dk_ref, ds_ref,
                      o_ref, sout_ref, s_scr, *, group):
    c = pl.program_id(1)
    nc = pl.num_programs(1)

    @pl.when(c == 0)
    def _():
        s_scr[...] = s0_ref[...]

    rows = q_ref.shape[1]

    def load(ref, g, hs):
        x = ref[g, :, hs].astype(F32)
        if rows < RET_CHUNK:
            x = jnp.concatenate([x, jnp.zeros((RET_CHUNK - rows, LANES), F32)], axis=0)
        return x

    for g in range(group):
        for h in range(RET_HEADS):
            hs = slice(h * LANES, (h + 1) * LANES)
            q = load(q_ref, g, hs)
            k = load(k_ref, g, hs)
            v = load(v_ref, g, hs).astype(BF16)
            s_old = s_scr[g, h]
            inner = lax.dot_general(q.astype(BF16), k.astype(BF16), NT_DIMS,
                                    preferred_element_type=F32) * di_ref[h]
            o = (jnp.dot(inner.astype(BF16), v, preferred_element_type=F32)
                 + jnp.dot((q * dq_ref[:, hs]).astype(BF16), s_old.astype(BF16),
                           preferred_element_type=F32))
            kd = (k * dk_ref[:, hs]).T.astype(BF16)
            s_scr[g, h] = s_old * ds_ref[:, hs] + jnp.dot(kd, v, preferred_element_type=F32)
            mu = jnp.mean(o, axis=-1, keepdims=True)
            cen = o - mu
            var = jnp.mean(cen * cen, axis=-1, keepdims=True)
            gn = cen * lax.rsqrt(var + GN_EPS)
            o_ref[g, :, hs] = (gn[:rows] * _silu(z_ref[g, :, hs].astype(F32))).astype(o_ref.dtype)

    @pl.when(c == nc - 1)
    def _():
        sout_ref[...] = s_scr[...]


def _retention(rq, rk, rv, rz, s0_all, layer, decay, nb, nchunks, rows, out_dtype):
    di, dq, dk, ds = decay
    c = RET_CHUNK
    group = max(g for g in (4, 2, 1) if nb % g == 0)
    blk = pl.BlockSpec((group, rows, 512), lambda b, j: (b, j, 0))
    st_in = pl.BlockSpec((None, group, RET_HEADS, RET_DK, RET_DV), lambda b, j: (layer, b, 0, 0, 0))
    st = pl.BlockSpec((group, RET_HEADS, RET_DK, RET_DV), lambda b, j: (b, 0, 0, 0))
    const2 = lambda b, j: (0, 0)
    return pl.pallas_call(
        functools.partial(_retention_kernel, group=group),
        grid=(nb // group, nchunks),
        in_specs=[blk, blk, blk, blk, st_in,
                  pl.BlockSpec((RET_HEADS, c, c), lambda b, j: (0, 0, 0)),
                  pl.BlockSpec((c, 512), const2),
                  pl.BlockSpec((c, 512), const2),
                  pl.BlockSpec((1, 512), const2)],
        out_specs=[blk, st],
        out_shape=[jax.ShapeDtypeStruct((nb, nchunks * rows, 512), out_dtype),
                   jax.ShapeDtypeStruct((nb, RET_HEADS, RET_DK, RET_DV), F32)],
        scratch_shapes=[pltpu.VMEM((group, RET_HEADS, RET_DK, RET_DV), F32)],
        compiler_params=pltpu.CompilerParams(dimension_semantics=("arbitrary", "arbitrary"),
                                             vmem_limit_bytes=VMEM_LIMIT),
        name="retention",
    )(rq, rk, rv, rz, s0_all, di, dq, dk, ds)


KEY_NEG_INF = -2139095041


def _cand_float(key):
    bits = key ^ (lax.shift_right_arithmetic(key, 31) & 0x7FFFFFFF)
    return jnp.where(key < KEY_NEG_INF, -jnp.inf, pltpu.bitcast(bits, F32))


def _split_heads(x2, rows):
    lane = lax.broadcasted_iota(I32, (rows, LANES), 1)
    lo = lane < ATT_HEAD_DIM
    return jnp.where(lo, x2, 0.0), jnp.where(lo, 0.0, x2)


def _indexer_scores(iqs, ikc, iwb, rows, feature_major=False):
    if feature_major:
        lg = jnp.dot(iqs, ikc, preferred_element_type=F32)
    else:
        lg = lax.dot_general(iqs, ikc, NT_DIMS, preferred_element_type=F32)
    sc = None
    for h in range(IDX_HEADS):
        t = jnp.maximum(lg[h * rows:(h + 1) * rows] * (IDX_DIM ** -0.5), 0.0) * iwb[h]
        sc = t if sc is None else sc + t
    return sc


def _rows8(x):
    return x.reshape(x.shape[0] // 8, 8, LANES)


def _fold8(x8, op):
    return jnp.broadcast_to(op(x8, axis=0, keepdims=True), (8, LANES))


def _reduce0(x3, op, ways=8):
    accs = [x3[j] for j in range(ways)]
    for j in range(ways, x3.shape[0]):
        accs[j % ways] = op(accs[j % ways], x3[j])
    while len(accs) > 1:
        accs = [op(accs[a], accs[a + 1]) for a in range(0, len(accs), 2)]
    return accs[0]


I16 = jnp.int16
I16_MIN = -(2 ** 15)


def _chunk_loop(nchunks, body, init):
    if isinstance(nchunks, int):
        for c in range(nchunks):
            init = body(c, init)
        return init
    return lax.fori_loop(0, nchunks, body, init)


def _chunk_start(c, kc):
    return c * kc if isinstance(c, int) else pl.multiple_of(c * kc, kc)


def _count_f(sc_ref, nchunks, kc, cand, strict, ways=4):
    def body(c, accs):
        x = _rows8(sc_ref[pl.ds(_chunk_start(c, kc), kc), :])
        ones = jnp.where((x > cand[None]) if strict else (x >= cand[None]), 1.0, 0.0)
        accs = list(accs)
        for j in range(kc // 8):
            accs[j % ways] = accs[j % ways] + ones[j]
        return tuple(accs)

    accs = _chunk_loop(nchunks, body, tuple(jnp.zeros((8, LANES), F32) for _ in range(ways)))
    return _fold8((accs[0] + accs[1]) + (accs[2] + accs[3]), jnp.sum)


def _count_b(sb_ref, nchunks, kc, cand, ways=4):
    c16 = jnp.concatenate([cand, cand], axis=0).astype(BF16)[None]

    def body(c, accs):
        x = sb_ref[pl.ds(_chunk_start(c, kc), kc), :].reshape(kc // 16, 16, LANES)
        ones = jnp.where(x >= c16, jnp.int16(1), jnp.int16(0))
        accs = list(accs)
        for j in range(kc // 16):
            accs[j % ways] = accs[j % ways] + ones[j]
        return tuple(accs)

    accs = _chunk_loop(nchunks, body, tuple(jnp.zeros((16, LANES), I16) for _ in range(ways)))
    tot = ((accs[0] + accs[1]) + (accs[2] + accs[3])).astype(I32).astype(F32)
    return jnp.broadcast_to(jnp.sum(tot, axis=0, keepdims=True), (8, LANES))


def _kth_largest_t(sc_ref, sb_ref, nchunks, kc, topk):
    kf = float(topk)
    zero = jnp.zeros((8, LANES), I32)
    c0 = _count_b(sb_ref, nchunks, kc, _cand_float(zero))
    h = jnp.where(c0 >= kf, 0, I16_MIN).astype(I32)

    def hi_step(b, h):
        cand = h | lax.shift_left(jnp.int32(1), 14 - b)
        cnt = _count_b(sb_ref, nchunks, kc, _cand_float(cand * 65536))
        return jnp.where(cnt >= kf, cand, h)

    h = lax.fori_loop(0, 15, hi_step, h)
    base = jnp.maximum(h, I16_MIN + 1) * 65536 - 32768

    def lo_step(b, o):
        cand = o | lax.shift_left(jnp.int32(1), 16 - b)
        cnt = _count_f(sc_ref, nchunks, kc, _cand_float(base + cand), False)
        return jnp.where(cnt >= kf, cand, o)

    tau = _cand_float(base + lax.fori_loop(0, 17, lo_step, zero))
    return tau, kf - _count_f(sc_ref, nchunks, kc, tau, True)


def _select_bias_t(kk3, tau, need, tie_carry, tril, kc):
    eqf = jnp.where(kk3 == tau[None], 1.0, 0.0).reshape(kc, LANES).astype(BF16)
    ranks = []
    for blk in range(kc // LANES):
        r = jnp.dot(tril, eqf[blk * LANES:(blk + 1) * LANES], preferred_element_type=F32)
        ranks.append(_rows8(r) + tie_carry[None])
        tie_carry = tie_carry + jnp.broadcast_to(r[LANES - 1:LANES, :], (8, LANES))
    rank = jnp.concatenate(ranks, axis=0)
    tie_ok = jnp.where(rank <= need[None], 0.0, NEG_BIG)
    bias = jnp.where(kk3 > tau[None], 0.0, jnp.where(kk3 == tau[None], tie_ok, NEG_BIG))
    return bias, tie_carry


def _attn_prompt_kernel(aq_ref, iq_ref, iw_ref, az_ref, kb_ref, vt_ref, ik_ref, tril_ref, ga_ref,
                        sc_ref, sb_ref, m_ref, acc_ref, *, topk):
    tq, kc = Q_TILE, KEY_CHUNK
    i = pl.program_id(1)
    nkc = lax.div(i * tq + tq + kc - 1, kc)
    qpos = i * tq + lax.broadcasted_iota(I32, (kc, tq), 1)
    krow = lax.broadcasted_iota(I32, (kc, tq), 0)

    iq = iq_ref[...].astype(F32) * (IDX_DIM ** -0.5)
    parts = []
    for s in range(2):
        parts += list(_split_heads(iq[:, s * LANES:(s + 1) * LANES], tq))
    iqs = jnp.concatenate(parts, axis=0).astype(BF16)
    iwt = (iw_ref[...] * (IDX_HEADS ** -0.5)).T

    def scores(c, carry):
        k0 = _chunk_start(c, kc)
        ikc = ik_ref[pl.ds(k0, kc), :]
        sc = None
        for pr in range(IDX_HEADS // 2):
            lg = lax.dot_general(ikc, iqs[pr * 2 * tq:(pr + 1) * 2 * tq], NT_DIMS,
                                 preferred_element_type=F32)
            for hh in range(2):
                h = 2 * pr + hh
                t = jnp.maximum(lg[:, hh * tq:(hh + 1) * tq], 0.0) * iwt[h:h + 1, :]
                sc = t if sc is None else sc + t
        sc = jnp.where(k0 + krow <= qpos, sc, -jnp.inf)
        sc_ref[pl.ds(k0, kc), :] = sc
        sb_ref[pl.ds(k0, kc), :] = sc.astype(BF16)
        return carry

    aq = aq_ref[...].astype(F32) * (ATT_HEAD_DIM ** -0.5 * LOG2E)
    lo_heads, hi_heads = [], []
    for s in range(4):
        lo, hi = _split_heads(aq[:, s * LANES:(s + 1) * LANES], tq)
        lo_heads.append(lo)
        hi_heads.append(hi)
    qs = jnp.concatenate(lo_heads + hi_heads, axis=0).astype(BF16)

    m_ref[...] = jnp.full(m_ref.shape, NEG_BIG, F32)
    acc_ref[...] = jnp.zeros(acc_ref.shape, F32)
    tril = tril_ref[...]

    def attend(tau, need, c, tie_carry):
        k0 = _chunk_start(c, kc)
        bias, tie_carry = _select_bias_t(_rows8(sc_ref[pl.ds(k0, kc), :]), tau, need, tie_carry,
                                         tril, kc)
        kcb = kb_ref[pl.ds(k0, kc), :]
        p_cols, alphas = [], []
        for pr in range(ATT_HEADS // 2):
            s = lax.dot_general(kcb, qs[pr * 2 * tq:(pr + 1) * 2 * tq], NT_DIMS,
                                preferred_element_type=F32)
            for hh in range(2):
                h = 2 * pr + hh
                sh = _rows8(s[:, hh * tq:(hh + 1) * tq]) + bias
                m_old = m_ref[h]
                m_new = jnp.maximum(m_old, _fold8(_reduce0(sh, jnp.maximum), jnp.max))
                alpha = jnp.exp2(m_old - m_new)
                p = jnp.exp2(sh - m_new[None])
                m_ref[h] = m_new
                p_cols.append(p.reshape(kc, tq).astype(BF16))
                alphas.append(alpha[0:1, :])
        pv = jnp.dot(vt_ref[:, pl.ds(k0, kc)], jnp.concatenate(p_cols, axis=1),
                     preferred_element_type=F32)
        acc_ref[...] = jnp.concatenate(alphas, axis=1) * acc_ref[...] + pv
        return tie_carry

    def select(n):
        _chunk_loop(n, scores, 0)
        return _kth_largest_t(sc_ref, sb_ref, n, kc, topk)

    tau, need = lax.switch(nkc - 1, [functools.partial(select, n)
                                     for n in range(1, sc_ref.shape[0] // kc + 1)])
    need = jnp.where(tau == -jnp.inf, 0.0, need)
    sweep = functools.partial(attend, tau, need)

    def sweep2(j, carry):
        return sweep(2 * j + 1, sweep(2 * j, carry))

    carry = lax.fori_loop(0, lax.div(nkc, 2), sweep2, jnp.zeros((8, LANES), F32))

    @pl.when(lax.rem(nkc, 2) == 1)
    def _():
        sweep(nkc - 1, carry)

    frow = lax.broadcasted_iota(I32, (LANES, tq), 0)

    def head_out(h):
        cols = slice(h * tq, (h + 1) * tq)
        return acc_ref[0:LANES, cols] / acc_ref[LANES:LANES + 1, cols]

    for s in range(4):
        o = jnp.where(frow < ATT_HEAD_DIM, head_out(s), head_out(s + 4)).T
        az = az_ref[:, s * LANES:(s + 1) * LANES].astype(F32)
        ga_ref[:, s * LANES:(s + 1) * LANES] = (o * _silu(az)).astype(ga_ref.dtype)


def _attn_prompt(aq, iq, iw, az, akb, avt, ikk, tril, nb, t, topk):
    n = aq.shape[0]
    nq = t // Q_TILE
    qrow = lambda b, i: (b * nq + i, 0)
    seq = lambda b, i: (b, 0)
    return pl.pallas_call(
        functools.partial(_attn_prompt_kernel, topk=topk),
        grid=(nb, nq),
        in_specs=[pl.BlockSpec((Q_TILE, 512), qrow),
                  pl.BlockSpec((Q_TILE, 256), qrow),
                  pl.BlockSpec((Q_TILE, LANES), qrow),
                  pl.BlockSpec((Q_TILE, 512), qrow),
                  pl.BlockSpec((t, LANES), seq),
                  pl.BlockSpec((VT_ROWS, t), lambda b, i: (0, b)),
                  pl.BlockSpec((t, LANES), seq),
                  pl.BlockSpec((LANES, LANES), lambda b, i: (0, 0))],
        out_specs=pl.BlockSpec((Q_TILE, 512), qrow),
        out_shape=jax.ShapeDtypeStruct((n, 512), aq.dtype),
        scratch_shapes=[pltpu.VMEM((t, Q_TILE), F32),
                        pltpu.VMEM((t, Q_TILE), BF16),
                        pltpu.VMEM((ATT_HEADS, 8, Q_TILE), F32),
                        pltpu.VMEM((VT_ROWS, ATT_HEADS * Q_TILE), F32)],
        compiler_params=pltpu.CompilerParams(dimension_semantics=("arbitrary", "arbitrary"),
                                             vmem_limit_bytes=VMEM_LIMIT),
        name="attn_prompt",
    )(aq, iq, iw, az, akb, avt, ikk, tril)


def _sample_scores_kernel(pt_ref, iq_ref, iw_ref, iknew_ref, *rest, n_pages, page, ts):
    del pt_ref
    page_refs, sc_ref = rest[:n_pages], rest[n_pages]
    past = n_pages * page
    iq = iq_ref[...].astype(F32)
    heads = [iq[:, h * IDX_DIM:(h + 1) * IDX_DIM] for h in range(IDX_HEADS)]
    iqs = jnp.concatenate(heads, axis=0).astype(BF16)
    iw = iw_ref[...] * (IDX_HEADS ** -0.5)
    iwb_past = [jnp.broadcast_to(iw[:, h:h + 1], (ts, past)) for h in range(IDX_HEADS)]
    iwb = [w[:, :page] for w in iwb_past]
    ikt = jnp.concatenate([r[...].astype(BF16) for r in page_refs], axis=1)
    sc_ref[:, :past] = _indexer_scores(iqs, ikt, iwb_past, ts, feature_major=True)
    new = jnp.concatenate([iknew_ref[...], jnp.zeros((page - ts, IDX_DIM), F32)], axis=0)
    sc = _indexer_scores(iqs, new.astype(BF16), iwb, ts)
    row = lax.broadcasted_iota(I32, (ts, page), 0)
    col = lax.broadcasted_iota(I32, (ts, page), 1)
    sc_ref[:, past:past + page] = jnp.where(col <= row, sc, -jnp.inf)


def _sample_scores(page_table, iq, iw, ik_new, cache_idx, layer, ts):
    nb, n_pages = page_table.shape
    page = cache_idx.shape[3]
    lp = (n_pages + 1) * page
    row = lambda b, pt: (b, 0)
    page_specs = [pl.BlockSpec((None, None, IDX_DIM, page),
                               functools.partial(lambda b, pt, p: (layer, pt[b, p], 0, 0), p=p))
                  for p in range(n_pages)]
    grid_spec = pltpu.PrefetchScalarGridSpec(
        num_scalar_prefetch=1,
        grid=(nb,),
        in_specs=[pl.BlockSpec((ts, 256), row),
                  pl.BlockSpec((ts, LANES), row),
                  pl.BlockSpec((ts, IDX_DIM), row)] + page_specs,
        out_specs=pl.BlockSpec((ts, lp), row),
    )
    return pl.pallas_call(
        functools.partial(_sample_scores_kernel, n_pages=n_pages, page=page, ts=ts),
        grid_spec=grid_spec,
        out_shape=jax.ShapeDtypeStruct((nb * ts, lp), F32),
        compiler_params=pltpu.CompilerParams(dimension_semantics=("arbitrary",),
                                             vmem_limit_bytes=VMEM_LIMIT),
        name="sample_scores",
    )(page_table, iq, iw, ik_new, *([cache_idx] * n_pages))


def _sample_select_kernel(sc_in_ref, tril_ref, bias_ref, sc_ref, sb_ref, *, topk, lp):
    nch = lp // LANES
    for c in range(nch):
        rows = slice(c * LANES, (c + 1) * LANES)
        t = sc_in_ref[:, rows].T
        sc_ref[rows, :] = t
        sb_ref[rows, :] = t.astype(BF16)
    tau, need = _kth_largest_t(sc_ref, sb_ref, nch, LANES, topk)
    need = jnp.where(tau == -jnp.inf, 0.0, need)
    tril = tril_ref[...]
    tie_carry = jnp.zeros((8, LANES), F32)
    for c in range(nch):
        rows = slice(c * LANES, (c + 1) * LANES)
        bias, tie_carry = _select_bias_t(_rows8(sc_ref[rows, :]), tau, need, tie_carry, tril, LANES)
        bias_ref[:, rows] = bias.reshape(LANES, LANES).T


def _sample_select(scores, tril, topk):
    n, lp = scores.shape
    rows = LANES
    return pl.pallas_call(
        functools.partial(_sample_select_kernel, topk=topk, lp=lp),
        grid=(n // rows,),
        in_specs=[pl.BlockSpec((rows, lp), lambda i: (i, 0)),
                  pl.BlockSpec((LANES, LANES), lambda i: (0, 0))],
        out_specs=pl.BlockSpec((rows, lp), lambda i: (i, 0)),
        out_shape=jax.ShapeDtypeStruct((n, lp), F32),
        scratch_shapes=[pltpu.VMEM((lp, LANES), F32), pltpu.VMEM((lp, LANES), BF16)],
        compiler_params=pltpu.CompilerParams(dimension_semantics=("arbitrary",),
                                             vmem_limit_bytes=VMEM_LIMIT),
        name="sample_select",
    )(scores, tril)


def _sample_attn_kernel(pt_ref, aq_ref, az_ref, bias_ref, knew_ref, vnew_ref, *rest,
                        n_pages, page, ts):
    del pt_ref
    k_refs, v_refs = rest[:n_pages], rest[n_pages:2 * n_pages]
    ga_ref = rest[2 * n_pages]
    aq = aq_ref[...].astype(F32) * (ATT_HEAD_DIM ** -0.5)
    lo_heads, hi_heads = [], []
    for s in range(4):
        lo, hi = _split_heads(aq[:, s * LANES:(s + 1) * LANES], ts)
        lo_heads.append(lo)
        hi_heads.append(hi)
    qs = jnp.concatenate(lo_heads + hi_heads, axis=0).astype(BF16)
    pad = jnp.zeros((page - ts, LANES), F32)
    k_new = jnp.concatenate([knew_ref[...], pad], axis=0).astype(BF16)
    v_new = jnp.concatenate([vnew_ref[...], pad], axis=0).astype(BF16)

    past = n_pages * page
    kt = jnp.concatenate([r[...].astype(BF16) for r in k_refs], axis=1)
    vt = jnp.concatenate([r[...].astype(BF16) for r in v_refs], axis=1)
    s = jnp.concatenate([jnp.dot(qs, kt, preferred_element_type=F32),
                         lax.dot_general(qs, k_new, NT_DIMS, preferred_element_type=F32)], axis=1)
    s = s + jnp.concatenate([bias_ref[...]] * ATT_HEADS, axis=0)
    p = jnp.exp(s - jnp.max(s, axis=1, keepdims=True))
    denom = jnp.sum(p, axis=1, keepdims=True)
    pb = p.astype(BF16)
    acc = (lax.dot_general(pb[:, :past], vt, NT_DIMS, preferred_element_type=F32)
           + jnp.dot(pb[:, past:], v_new, preferred_element_type=F32))
    o = acc / denom
    lane = lax.broadcasted_iota(I32, (ts, LANES), 1)
    for s in range(4):
        oo = jnp.where(lane < ATT_HEAD_DIM, o[s * ts:(s + 1) * ts], o[(s + 4) * ts:(s + 5) * ts])
        az = az_ref[:, s * LANES:(s + 1) * LANES].astype(F32)
        ga_ref[:, s * LANES:(s + 1) * LANES] = (oo * _silu(az)).astype(ga_ref.dtype)


def _sample_attn(page_table, aq, az, bias, k_new, v_new, cache_k, cache_v, layer, ts):
    nb, n_pages = page_table.shape
    page = cache_k.shape[3]
    lp = (n_pages + 1) * page
    row = lambda b, pt: (b, 0)
    page_specs = [pl.BlockSpec((None, None, LANES, page),
                               functools.partial(lambda b, pt, p: (layer, pt[b, p], 0, 0), p=p))
                  for p in range(n_pages)]
    grid_spec = pltpu.PrefetchScalarGridSpec(
        num_scalar_prefetch=1,
        grid=(nb,),
        in_specs=[pl.BlockSpec((ts, 512), row),
                  pl.BlockSpec((ts, 512), row),
                  pl.BlockSpec((ts, lp), row),
                  pl.BlockSpec((ts, LANES), row),
                  pl.BlockSpec((ts, LANES), row)] + page_specs + page_specs,
        out_specs=pl.BlockSpec((ts, 512), row),
    )
    return pl.pallas_call(
        functools.partial(_sample_attn_kernel, n_pages=n_pages, page=page, ts=ts),
        grid_spec=grid_spec,
        out_shape=jax.ShapeDtypeStruct((nb * ts, 512), aq.dtype),
        compiler_params=pltpu.CompilerParams(dimension_semantics=("arbitrary",),
                                             vmem_limit_bytes=VMEM_LIMIT),
        name="sample_attn",
    )(page_table, aq, az, bias, k_new, v_new, *([cache_k] * n_pages), *([cache_v] * n_pages))


def _merge_kernel(x_ref, p_ref, gr_ref, ga_ref, br_ref, ba_ref, wor_ref, woa_ref, wout_ref,
                  wpg_ref, wpp_ref, y_ref):
    u_r = jnp.dot(br_ref[...].astype(BF16), wor_ref[...], preferred_element_type=F32)
    u_a = jnp.dot(ba_ref[...].astype(BF16), woa_ref[...], preferred_element_type=F32)
    m = (jax.nn.sigmoid(gr_ref[...].astype(F32)) * u_r
         + jax.nn.sigmoid(ga_ref[...].astype(F32)) * u_a)
    x1 = x_ref[...] + jnp.dot(m.astype(BF16), wout_ref[...], preferred_element_type=F32)
    gate = jax.nn.sigmoid(jnp.dot(x1.astype(BF16), wpg_ref[...], preferred_element_type=F32))
    y_ref[...] = x1 + gate * jnp.dot(p_ref[...].astype(BF16), wpp_ref[...],
                                     preferred_element_type=F32)


def _merge(x, p_all, layer, gr, ga, br, ba, weights, tm):
    n, d = x.shape
    row = lambda i: (i, 0)
    wspec = lambda w: pl.BlockSpec((None,) + w.shape[1:], lambda i: (layer, 0, 0))
    return pl.pallas_call(
        _merge_kernel,
        grid=(n // tm,),
        in_specs=[pl.BlockSpec((tm, d), row),
                  pl.BlockSpec((None, tm, p_all.shape[2]), lambda i: (layer, i, 0)),
                  pl.BlockSpec((tm, d), row),
                  pl.BlockSpec((tm, d), row),
                  pl.BlockSpec((tm, 512), row),
                  pl.BlockSpec((tm, 512), row)] + [wspec(w) for w in weights],
        out_specs=pl.BlockSpec((tm, d), row),
        out_shape=jax.ShapeDtypeStruct((n, d), F32),
        compiler_params=pltpu.CompilerParams(dimension_semantics=("arbitrary",),
                                             vmem_limit_bytes=VMEM_LIMIT),
        name="merge",
    )(x, p_all, gr, ga, br, ba, *weights)


def _pair_heads_rows(m):
    d = m.shape[1]
    m = m.reshape(ATT_KV_HEADS, ATT_HEADS // ATT_KV_HEADS, ATT_HEAD_DIM, d)
    return jnp.concatenate([m[0], m[1]], axis=1).reshape(ATT_HEADS * ATT_HEAD_DIM, d)


def _pack_w_in(wt):
    sizes = (512, 512, 512, 512, 512, 128, 128, 512, 256, 64, 4, 1024, 1024)
    offs, o = [], 0
    for s in sizes:
        offs.append((o, o + s))
        o += s
    rq, rk, rv, rz, aq, ak, av, az, iq, ik, iw, gr, ga = [wt[a:b] for a, b in offs]
    iwp = jnp.pad(iw, ((0, LANES - IDX_HEADS), (0, 0)))
    packed = jnp.concatenate([rq, rk, rv, rz, _pair_heads_rows(aq), _pair_heads_rows(az), ak, av,
                              iq, ik, ik, iwp, gr, ga], axis=0)
    return packed.astype(BF16)


def _rope_tables(pos):
    pf = pos.astype(F32)[:, None]
    half = RET_DK // 2
    freqs = jnp.exp(-math.log(RET_THETA) * jnp.arange(half, dtype=F32) / half)
    ang = pf * freqs[None, :]
    cos, sin = jnp.cos(ang), jnp.sin(ang)
    rc = jnp.concatenate([cos, cos], axis=1)
    rs = jnp.concatenate([-sin, sin], axis=1)
    half = ROPE_DIM // 2
    freqs = jnp.exp(-math.log(ROPE_THETA) * jnp.arange(half, dtype=F32) / half)
    ang = pf * freqs[None, :]
    cos, sin = jnp.cos(ang), jnp.sin(ang)
    r = pos.shape[0]
    rest = ATT_HEAD_DIM - ROPE_DIM
    one, zero, zh = jnp.ones((r, rest), F32), jnp.zeros((r, rest), F32), jnp.zeros((r, half), F32)
    ac = jnp.concatenate([cos, cos, one], axis=1)
    as1 = jnp.concatenate([-sin, zh, zero], axis=1)
    as2 = jnp.concatenate([zh, sin, zero], axis=1)
    tile2 = lambda a: jnp.concatenate([a, a], axis=1)
    return rc, rs, tile2(ac), tile2(as1), tile2(as2)


def _decay_tables(c_eff):
    h = RET_HEADS
    log_g = jnp.log1p(-jnp.exp2(-5.0 - jnp.arange(h, dtype=F32)))
    c = jnp.arange(RET_CHUNK, dtype=F32)
    diff = c[:, None] - c[None, :]
    di = jnp.where(diff[None] >= 0, jnp.exp(jnp.maximum(diff, 0.0)[None] * log_g[:, None, None]), 0.0)
    dq = jnp.exp((c[:, None] + 1.0) * log_g[None, :])
    dk = jnp.exp((c_eff - 1.0 - c)[:, None] * log_g[None, :])
    dk = jnp.where(c[:, None] < c_eff, dk, 0.0)
    ds = jnp.exp(c_eff * log_g)
    rep = lambda a: jnp.repeat(a, RET_DK, axis=-1)
    return di, rep(dq), rep(dk), rep(ds[None, :])


def _tril(n):
    r = lax.broadcasted_iota(I32, (n, n), 0)
    c = lax.broadcasted_iota(I32, (n, n), 1)
    return jnp.where(c <= r, 1.0, 0.0).astype(BF16)


def _group_sum_matrix():
    r = lax.broadcasted_iota(I32, (LANES, LANES), 0) // ATT_HEAD_DIM
    c = lax.broadcasted_iota(I32, (LANES, LANES), 1) // ATT_HEAD_DIM
    return jnp.where(r == c, 1.0, 0.0).astype(BF16)


def _layer_weights(i, norm_gain, w_in_t, q_norm_gain, k_norm_gain):
    return dict(
        gain=norm_gain[i][None, :],
        w_in=_pack_w_in(w_in_t[:, i, :]),
        qg=jnp.tile(q_norm_gain[i], 2)[None, :],
        kg=jnp.tile(k_norm_gain[i], 2)[None, :],
    )


def _merge_weights(w_o_ret, w_o_att, w_out, w_ple_gate, w_ple_proj):
    depth, _, d = w_o_att.shape
    woa = w_o_att.reshape(depth, ATT_KV_HEADS, ATT_HEADS // ATT_KV_HEADS, ATT_HEAD_DIM, d)
    woa = jnp.concatenate([woa[:, 0], woa[:, 1]], axis=2).reshape(depth, -1, d)
    return tuple(w.astype(BF16) for w in (w_o_ret, woa, w_out, w_ple_gate, w_ple_proj))


def kernel(x_prompt, x_sample, cache_k, cache_v, cache_idx_k, state_ret, page_table, p_prompt,
           p_sample, norm_gain, w_in, q_norm_gain, k_norm_gain, w_o_ret, w_o_att, w_out, w_ple_gate,
           w_ple_proj):
    bp, tp, d = x_prompt.shape
    bs, ts, _ = x_sample.shape
    depth = w_in.shape[0]
    n_pool, page = cache_k.shape[1], cache_k.shape[2]
    n_pages = page_table.shape[1]
    past = n_pages * page
    topk_p = min(TOPK_MAX, tp // 4)
    topk_s = min(TOPK_MAX, (past + ts) // 4)
    assert tp % KEY_CHUNK == 0 and tp % RET_CHUNK == 0 and ts <= page and ts % 8 == 0
    assert topk_p <= KEY_CHUNK

    np_, ns_ = bp * tp, bs * ts
    tm_p = 512 if np_ % 512 == 0 else Q_TILE
    tm_s = 256 if ns_ % 256 == 0 else ns_
    assert tm_s % ts == 0 and tp % tm_p == 0

    gsum = _group_sum_matrix()
    tabs_p = _rope_tables(jnp.arange(tp, dtype=I32))
    tabs_s = _rope_tables(past + (jnp.arange(tm_s, dtype=I32) % ts))
    decay_p = _decay_tables(float(RET_CHUNK))
    decay_s = _decay_tables(float(ts))
    tril_p = _tril(LANES)
    ck = cache_k.reshape(depth, n_pool, page, LANES).transpose(0, 1, 3, 2)
    cv = cache_v.reshape(depth, n_pool, page, LANES).transpose(0, 1, 3, 2)
    cik = cache_idx_k.transpose(0, 1, 3, 2)
    w_in_t = w_in.transpose(2, 0, 1)
    s0_p = jnp.zeros((1, bp, RET_HEADS, RET_DK, RET_DV), F32)

    xp = x_prompt.reshape(np_, d)
    xs = x_sample.reshape(ns_, d)
    outs = {k: [] for k in ("kp", "vp", "ikp", "sp", "ks", "vs", "iks", "ss")}

    pp_all = p_prompt.reshape(depth, np_, -1)
    ps_all = p_sample.reshape(depth, ns_, -1)
    mw = _merge_weights(w_o_ret, w_o_att, w_out, w_ple_gate, w_ple_proj)

    for i in range(depth):
        lw = _layer_weights(i, norm_gain, w_in_t, q_norm_gain, k_norm_gain)

        (rq, rk, rv, rz, aq, az, akb, avt, iq, ikk, gr, ga, ak, av, ik, iw) = _inproj(
            xp, lw["gain"], lw["w_in"], tabs_p, lw["qg"], lw["kg"], gsum, tm_p, tp // tm_p, BF16)
        seq3 = lambda a: a.reshape(bp, tp, 512)
        b_r, s_new = _retention(seq3(rq), seq3(rk), seq3(rv), seq3(rz), s0_p, 0, decay_p, bp,
                                tp // RET_CHUNK, RET_CHUNK, BF16)
        b_r = b_r.reshape(np_, 512)
        b_a = _attn_prompt(aq, iq, iw, az, akb, avt, ikk, tril_p, bp, tp, topk_p)
        xp = _merge(xp, pp_all, i, gr, ga, b_r, b_a, mw, tm_p)
        outs["kp"].append(ak.reshape(bp, tp, ATT_KV_HEADS, ATT_HEAD_DIM))
        outs["vp"].append(av.reshape(bp, tp, ATT_KV_HEADS, ATT_HEAD_DIM))
        outs["ikp"].append(ik.reshape(bp, tp, IDX_DIM))
        outs["sp"].append(s_new)

        (rq, rk, rv, rz, aq, az, akb, avb, iq, ikk, gr, ga, ak, av, ik, iw) = _inproj(
            xs, lw["gain"], lw["w_in"], tabs_s, lw["qg"], lw["kg"], gsum, tm_s, 1, F32)
        tok3 = lambda a: a.reshape(bs, ts, 512)
        b_r, s_new = _retention(tok3(rq), tok3(rk), tok3(rv), tok3(rz), state_ret, i, decay_s, bs,
                                1, ts, F32)
        b_r = b_r.reshape(ns_, 512)
        keys = _sample_scores(page_table, iq, iw, ik, cik, i, ts)
        bias = _sample_select(keys, tril_p, topk_s)
        b_a = _sample_attn(page_table, aq, az, bias, ak, av, ck, cv, i, ts)
        xs = _merge(xs, ps_all, i, gr, ga, b_r, b_a, mw, tm_s)
        outs["ks"].append(ak.reshape(bs, ts, ATT_KV_HEADS, ATT_HEAD_DIM))
        outs["vs"].append(av.reshape(bs, ts, ATT_KV_HEADS, ATT_HEAD_DIM))
        outs["iks"].append(ik.reshape(bs, ts, IDX_DIM))
        outs["ss"].append(s_new)

    st = lambda k: jnp.stack(outs[k])
    return (xp.reshape(bp, tp, d), xs.reshape(bs, ts, d), st("kp"), st("vp"), st("ikp"), st("sp"),
            st("ks"), st("vs"), st("iks"), st("ss"))
```

```python
import functools
import math

import jax
import jax.numpy as jnp
from jax import lax
from jax.experimental import pallas as pl
from jax.experimental.pallas import tpu as pltpu

F32 = jnp.float32
BF16 = jnp.bfloat16
I32 = jnp.int32

RET_HEADS = 4
RET_DK = 128
RET_DV = 128
RET_CHUNK = 128
RET_BLOCK = 256
SAMPLE_GROUP = 4
DEC_PAD = 32
RET_THETA = 10000.0
ATT_HEADS = 8
ATT_KV_HEADS = 2
ATT_HEAD_DIM = 64
ROPE_THETA = 500000.0
ROPE_DIM = ATT_HEAD_DIM // 4
IDX_HEADS = 4
IDX_DIM = 64
TOPK_MAX = 256
NORM_EPS = 1e-6
GN_EPS = 1e-5

LANES = 128
Q_TILE = 128
KEY_CHUNK = 512
VMEM_LIMIT = 56 * 1024 * 1024
NEG_BIG = -1e30

C_RQ, C_RK, C_RV, C_RZ = 0, 512, 1024, 1536
C_AQ, C_AZ, C_AK, C_AV = 2048, 2560, 3072, 3200
C_IQ, C_IKK, C_IW, C_GR, C_GA = 3328, 3584, 3712, 3840, 4864
W_PACKED = 5888

NT_DIMS = (((1,), (1,)), ((), ()))
VT_ROWS = LANES + 16
LOG2E = math.log2(math.e)


def _silu(x):
    return x * jax.nn.sigmoid(x)


def _inproj_kernel(x_ref, g_ref, w_ref, rc_ref, rs_ref, ac_ref, as1_ref, as2_ref, qg_ref, kg_ref,
                   gsum_ref,
                   rq_o, rk_o, rv_o, rz_o, aq_o, az_o, akb_o, avt_o, iq_o, ikk_o, gr_o, ga_o,
                   ak_o, av_o, ik_o, iw_o):
    sd = rq_o.dtype
    x = x_ref[...]
    ms = jnp.mean(x * x, axis=-1, keepdims=True)
    hb = ((x * lax.rsqrt(ms + NORM_EPS)) * g_ref[...]).astype(BF16)

    def mm(c0, width):
        return lax.dot_general(hb, w_ref[c0:c0 + width, :], NT_DIMS, preferred_element_type=F32)

    rc, rs = rc_ref[...], rs_ref[...]
    ac, as1, as2 = ac_ref[...], as1_ref[...], as2_ref[...]
    gsum = gsum_ref[...]

    def rope_ret(z):
        return z * rc + pltpu.roll(z, 64, 1) * rs

    def rope_att(z):
        return z * ac + pltpu.roll(z, LANES - 8, 1) * as1 + pltpu.roll(z, 8, 1) * as2

    def head_norm(z, gain):
        sq = z * z
        hi = sq.astype(BF16)
        lo = (sq - hi.astype(F32)).astype(BF16)
        ssq = (jnp.dot(hi, gsum, preferred_element_type=F32)
               + jnp.dot(lo, gsum, preferred_element_type=F32))
        return (z * lax.rsqrt(ssq * (1.0 / ATT_HEAD_DIM) + NORM_EPS)) * gain

    def slab(z, s):
        return z[:, s * LANES:(s + 1) * LANES]

    z = mm(C_RQ, 512)
    for s in range(4):
        rq_o[:, s * LANES:(s + 1) * LANES] = rope_ret(slab(z, s)).astype(sd)
    z = mm(C_RK, 512)
    for s in range(4):
        rk_o[:, s * LANES:(s + 1) * LANES] = (rope_ret(slab(z, s)) * (RET_DK ** -0.5)).astype(sd)
    rv_o[...] = mm(C_RV, 512).astype(sd)
    rz_o[...] = mm(C_RZ, 512).astype(sd)

    qg, kg = qg_ref[...], kg_ref[...]
    z = mm(C_AQ, 512)
    for s in range(4):
        aq_o[:, s * LANES:(s + 1) * LANES] = rope_att(head_norm(slab(z, s), qg)).astype(sd)
    az_o[...] = mm(C_AZ, 512).astype(sd)

    z = mm(C_AK, 256)
    k = rope_att(head_norm(slab(z, 0), kg))
    ak_o[...] = k
    akb_o[...] = k.astype(BF16)
    v = slab(z, 1)
    av_o[...] = v
    avt_o[...] = jnp.concatenate([v.T, jnp.ones((VT_ROWS - LANES, v.shape[0]), F32)],
                                 axis=0).astype(BF16)

    z = mm(C_IQ, 512)
    for s in range(2):
        iq_o[:, s * LANES:(s + 1) * LANES] = rope_att(slab(z, s)).astype(sd)
    ikk = rope_att(slab(z, 2))
    ikk_o[...] = ikk.astype(BF16)
    ik_o[...] = ikk[:, :IDX_DIM]
    iw_o[...] = slab(z, 3)

    gr_o[...] = mm(C_GR, 1024).astype(sd)
    ga_o[...] = mm(C_GA, 1024).astype(sd)


def _inproj(x, gain, w_packed, tabs, qg, kg, gsum, tm, pos_period_tiles, sd):
    n, d = x.shape
    grid = (n // tm,)
    row = lambda i: (i, 0)
    const = lambda i: (0, 0)
    tab = lambda i: (i % pos_period_tiles, 0)
    in_specs = [
        pl.BlockSpec((tm, d), row),
        pl.BlockSpec((1, d), const),
        pl.BlockSpec((W_PACKED, d), const),
    ] + [pl.BlockSpec((tm, LANES), tab)] * 5 + [
        pl.BlockSpec((1, LANES), const),
        pl.BlockSpec((1, LANES), const),
        pl.BlockSpec((LANES, LANES), const),
    ]
    widths = [(512, sd)] * 6 + [(128, BF16), (128, BF16), (256, sd), (128, BF16), (1024, sd), (1024, sd),
                                (128, F32), (128, F32), (IDX_DIM, F32), (128, F32)]
    out_shape = [jax.ShapeDtypeStruct((n, w), dt) for w, dt in widths]
    out_specs = [pl.BlockSpec((tm, w), row) for w, _ in widths]
    out_shape[7] = jax.ShapeDtypeStruct((VT_ROWS, n), BF16)
    out_specs[7] = pl.BlockSpec((VT_ROWS, tm), lambda i: (0, i))
    return pl.pallas_call(
        _inproj_kernel,
        grid=grid,
        in_specs=in_specs,
        out_specs=out_specs,
        out_shape=out_shape,
        compiler_params=pltpu.CompilerParams(dimension_semantics=("arbitrary",),
                                             vmem_limit_bytes=VMEM_LIMIT),
        name="inproj",
    )(x, gain, w_packed, *tabs, qg, kg, gsum)


def _retention_kernel(q_ref, k_ref, v_ref, z_ref, s0_ref, di_ref, dq_ref, dk_ref, ds_ref,
                      o_ref, sout_ref, s_scr, *, group):
    c = pl.program_id(1)
    nc = pl.num_programs(1)

    @pl.when(c == 0)
    def _():
        s_scr[...] = s0_ref[...]

    rows = q_ref.shape[1]
    cpad = di_ref.shape[1]

    def load(ref, g, hs):
        x = ref[g, :, hs].astype(F32)
        if rows < cpad:
            x = jnp.concatenate([x, jnp.zeros((cpad - rows, LANES), F32)], axis=0)
        return x

    for g in range(group):
        for h in range(RET_HEADS):
            hs = slice(h * LANES, (h + 1) * LANES)
            q = load(q_ref, g, hs)
            k = load(k_ref, g, hs)
            v = load(v_ref, g, hs).astype(BF16)
            s_old = s_scr[g, h]
            inner = lax.dot_general(q.astype(BF16), k.astype(BF16), NT_DIMS,
                                    preferred_element_type=F32) * di_ref[h]
            o = (jnp.dot(inner.astype(BF16), v, preferred_element_type=F32)
                 + jnp.dot((q * dq_ref[:, hs]).astype(BF16), s_old.astype(BF16),
                           preferred_element_type=F32))
            kd = (k * dk_ref[:, hs]).T.astype(BF16)
            s_scr[g, h] = s_old * ds_ref[:, hs] + jnp.dot(kd, v, preferred_element_type=F32)
            mu = jnp.mean(o, axis=-1, keepdims=True)
            cen = o - mu
            var = jnp.mean(cen * cen, axis=-1, keepdims=True)
            gn = cen * lax.rsqrt(var + GN_EPS)
            o_ref[g, :, hs] = (gn[:rows] * _silu(z_ref[g, :, hs].astype(F32))).astype(o_ref.dtype)

    @pl.when(c == nc - 1)
    def _():
        sout_ref[...] = s_scr[...]


def _retention(rq, rk, rv, rz, s0_all, layer, decay, nb, nchunks, rows, out_dtype):
    di, dq, dk, ds = decay
    c = di.shape[1]
    group = max(g for g in (8, 4, 2, 1) if nb % g == 0)
    blk = pl.BlockSpec((group, rows, 512), lambda b, j: (b, j, 0))
    st_in = pl.BlockSpec((None, group, RET_HEADS, RET_DK, RET_DV), lambda b, j: (layer, b, 0, 0, 0))
    st = pl.BlockSpec((group, RET_HEADS, RET_DK, RET_DV), lambda b, j: (b, 0, 0, 0))
    const2 = lambda b, j: (0, 0)
    return pl.pallas_call(
        functools.partial(_retention_kernel, group=group),
        grid=(nb // group, nchunks),
        in_specs=[blk, blk, blk, blk, st_in,
                  pl.BlockSpec((RET_HEADS, c, c), lambda b, j: (0, 0, 0)),
                  pl.BlockSpec((c, 512), const2),
                  pl.BlockSpec((c, 512), const2),
                  pl.BlockSpec((1, 512), const2)],
        out_specs=[blk, st],
        out_shape=[jax.ShapeDtypeStruct((nb, nchunks * rows, 512), out_dtype),
                   jax.ShapeDtypeStruct((nb, RET_HEADS, RET_DK, RET_DV), F32)],
        scratch_shapes=[pltpu.VMEM((group, RET_HEADS, RET_DK, RET_DV), F32)],
        compiler_params=pltpu.CompilerParams(dimension_semantics=("arbitrary", "arbitrary"),
                                             vmem_limit_bytes=VMEM_LIMIT),
        name="retention",
    )(rq, rk, rv, rz, s0_all, di, dq, dk, ds)


KEY_NEG_INF = -2139095041


def _cand_float(key):
    bits = key ^ (lax.shift_right_arithmetic(key, 31) & 0x7FFFFFFF)
    return jnp.where(key < KEY_NEG_INF, -jnp.inf, pltpu.bitcast(bits, F32))


def _split_heads(x2, rows):
    lane = lax.broadcasted_iota(I32, (rows, LANES), 1)
    lo = lane < ATT_HEAD_DIM
    return jnp.where(lo, x2, 0.0), jnp.where(lo, 0.0, x2)


def _indexer_scores(iqs, ikc, iwb, rows, feature_major=False):
    if feature_major:
        lg = jnp.dot(iqs, ikc, preferred_element_type=F32)
    else:
        lg = lax.dot_general(iqs, ikc, NT_DIMS, preferred_element_type=F32)
    sc = None
    for h in range(IDX_HEADS):
        t = jnp.maximum(lg[h * rows:(h + 1) * rows] * (IDX_DIM ** -0.5), 0.0) * iwb[h]
        sc = t if sc is None else sc + t
    return sc


def _rows8(x):
    return x.reshape(x.shape[0] // 8, 8, LANES)


def _fold8(x8, op):
    return jnp.broadcast_to(op(x8, axis=0, keepdims=True), (8, LANES))


def _reduce0(x3, op, ways=8):
    accs = [x3[j] for j in range(ways)]
    for j in range(ways, x3.shape[0]):
        accs[j % ways] = op(accs[j % ways], x3[j])
    while len(accs) > 1:
        accs = [op(accs[a], accs[a + 1]) for a in range(0, len(accs), 2)]
    return accs[0]


I16 = jnp.int16
I16_MIN = -(2 ** 15)


def _chunk_loop(nchunks, body, init):
    if isinstance(nchunks, int):
        for c in range(nchunks):
            init = body(c, init)
        return init
    return lax.fori_loop(0, nchunks, body, init)


def _chunk_start(c, kc):
    return c * kc if isinstance(c, int) else pl.multiple_of(c * kc, kc)


def _count_f(sc_ref, nchunks, kc, cand, strict, ways=4):
    def body(c, accs):
        x = _rows8(sc_ref[pl.ds(_chunk_start(c, kc), kc), :])
        ones = jnp.where((x > cand[None]) if strict else (x >= cand[None]), 1.0, 0.0)
        accs = list(accs)
        for j in range(kc // 8):
            accs[j % ways] = accs[j % ways] + ones[j]
        return tuple(accs)

    accs = _chunk_loop(nchunks, body, tuple(jnp.zeros((8, LANES), F32) for _ in range(ways)))
    return _fold8((accs[0] + accs[1]) + (accs[2] + accs[3]), jnp.sum)


def _count_b(sb_ref, nchunks, kc, cand, ways=4):
    c16 = jnp.concatenate([cand, cand], axis=0).astype(BF16)[None]

    def body(c, accs):
        x = sb_ref[pl.ds(_chunk_start(c, kc), kc), :].reshape(kc // 16, 16, LANES)
        ones = jnp.where(x >= c16, jnp.int16(1), jnp.int16(0))
        accs = list(accs)
        for j in range(kc // 16):
            accs[j % ways] = accs[j % ways] + ones[j]
        return tuple(accs)

    accs = _chunk_loop(nchunks, body, tuple(jnp.zeros((16, LANES), I16) for _ in range(ways)))
    tot = ((accs[0] + accs[1]) + (accs[2] + accs[3])).astype(I32).astype(F32)
    return jnp.broadcast_to(jnp.sum(tot, axis=0, keepdims=True), (8, LANES))


def _kth_largest_t(sc_ref, sb_ref, nchunks, kc, topk):
    kf = float(topk)
    zero = jnp.zeros((8, LANES), I32)
    c0 = _count_b(sb_ref, nchunks, kc, _cand_float(zero))
    h = jnp.where(c0 >= kf, 0, I16_MIN).astype(I32)

    def hi_step(b, h):
        cand = h | lax.shift_left(jnp.int32(1), 14 - b)
        cnt = _count_b(sb_ref, nchunks, kc, _cand_float(cand * 65536))
        return jnp.where(cnt >= kf, cand, h)

    h = lax.fori_loop(0, 15, hi_step, h)
    base = jnp.maximum(h, I16_MIN + 1) * 65536 - 32768

    def lo_step(b, o):
        cand = o | lax.shift_left(jnp.int32(1), 16 - b)
        cnt = _count_f(sc_ref, nchunks, kc, _cand_float(base + cand), False)
        return jnp.where(cnt >= kf, cand, o)

    tau = _cand_float(base + lax.fori_loop(0, 17, lo_step, zero))
    return tau, kf - _count_f(sc_ref, nchunks, kc, tau, True)


def _select_bias_t(kk3, tau, need, tie_carry, tril, kc):
    eqf = jnp.where(kk3 == tau[None], 1.0, 0.0).reshape(kc, LANES).astype(BF16)
    ranks = []
    for blk in range(kc // LANES):
        r = jnp.dot(tril, eqf[blk * LANES:(blk + 1) * LANES], preferred_element_type=F32)
        ranks.append(_rows8(r) + tie_carry[None])
        tie_carry = tie_carry + jnp.broadcast_to(r[LANES - 1:LANES, :], (8, LANES))
    rank = jnp.concatenate(ranks, axis=0)
    tie_ok = jnp.where(rank <= need[None], 0.0, NEG_BIG)
    bias = jnp.where(kk3 > tau[None], 0.0, jnp.where(kk3 == tau[None], tie_ok, NEG_BIG))
    return bias, tie_carry


def _attn_prompt_kernel(aq_ref, iq_ref, iw_ref, az_ref, kb_ref, vt_ref, ik_ref, tril_ref, ga_ref,
                        sc_ref, sb_ref, m_ref, acc_ref, *, topk):
    tq, kc = Q_TILE, KEY_CHUNK
    i = pl.program_id(1)
    nkc = lax.div(i * tq + tq + kc - 1, kc)
    qpos = i * tq + lax.broadcasted_iota(I32, (kc, tq), 1)
    krow = lax.broadcasted_iota(I32, (kc, tq), 0)

    iq = iq_ref[...].astype(F32) * (IDX_DIM ** -0.5)
    parts = []
    for s in range(2):
        parts += list(_split_heads(iq[:, s * LANES:(s + 1) * LANES], tq))
    iqs = jnp.concatenate(parts, axis=0).astype(BF16)
    iwt = (iw_ref[...] * (IDX_HEADS ** -0.5)).T

    def scores(c, carry):
        k0 = _chunk_start(c, kc)
        ikc = ik_ref[pl.ds(k0, kc), :]
        sc = None
        for pr in range(IDX_HEADS // 2):
            lg = lax.dot_general(ikc, iqs[pr * 2 * tq:(pr + 1) * 2 * tq], NT_DIMS,
                                 preferred_element_type=F32)
            for hh in range(2):
                h = 2 * pr + hh
                t = jnp.maximum(lg[:, hh * tq:(hh + 1) * tq], 0.0) * iwt[h:h + 1, :]
                sc = t if sc is None else sc + t
        sc = jnp.where(k0 + krow <= qpos, sc, -jnp.inf)
        sc_ref[pl.ds(k0, kc), :] = sc
        sb_ref[pl.ds(k0, kc), :] = sc.astype(BF16)
        return carry

    aq = aq_ref[...].astype(F32) * (ATT_HEAD_DIM ** -0.5 * LOG2E)
    lo_heads, hi_heads = [], []
    for s in range(4):
        lo, hi = _split_heads(aq[:, s * LANES:(s + 1) * LANES], tq)
        lo_heads.append(lo)
        hi_heads.append(hi)
    qs = jnp.concatenate(lo_heads + hi_heads, axis=0).astype(BF16)

    m_ref[...] = jnp.full(m_ref.shape, NEG_BIG, F32)
    acc_ref[...] = jnp.zeros(acc_ref.shape, F32)
    tril = tril_ref[...]

    def attend(tau, need, c, tie_carry):
        k0 = _chunk_start(c, kc)
        bias, tie_carry = _select_bias_t(_rows8(sc_ref[pl.ds(k0, kc), :]), tau, need, tie_carry,
                                         tril, kc)
        kcb = kb_ref[pl.ds(k0, kc), :]
        p_cols, alphas = [], []
        for pr in range(ATT_HEADS // 2):
            s = lax.dot_general(kcb, qs[pr * 2 * tq:(pr + 1) * 2 * tq], NT_DIMS,
                                preferred_element_type=F32)
            for hh in range(2):
                h = 2 * pr + hh
                sh = _rows8(s[:, hh * tq:(hh + 1) * tq]) + bias
                m_old = m_ref[h]
                m_new = jnp.maximum(m_old, _fold8(_reduce0(sh, jnp.maximum), jnp.max))
                alpha = jnp.exp2(m_old - m_new)
                p = jnp.exp2(sh - m_new[None])
                m_ref[h] = m_new
                p_cols.append(p.reshape(kc, tq).astype(BF16))
                alphas.append(alpha[0:1, :])
        pv = jnp.dot(vt_ref[:, pl.ds(k0, kc)], jnp.concatenate(p_cols, axis=1),
                     preferred_element_type=F32)
        acc_ref[...] = jnp.concatenate(alphas, axis=1) * acc_ref[...] + pv
        return tie_carry

    def select(n):
        _chunk_loop(n, scores, 0)
        return _kth_largest_t(sc_ref, sb_ref, n, kc, topk)

    tau, need = lax.switch(nkc - 1, [functools.partial(select, n)
                                     for n in range(1, sc_ref.shape[0] // kc + 1)])
    need = jnp.where(tau == -jnp.inf, 0.0, need)
    sweep = functools.partial(attend, tau, need)

    def sweep2(j, carry):
        return sweep(2 * j + 1, sweep(2 * j, carry))

    carry = lax.fori_loop(0, lax.div(nkc, 2), sweep2, jnp.zeros((8, LANES), F32))

    @pl.when(lax.rem(nkc, 2) == 1)
    def _():
        sweep(nkc - 1, carry)

    frow = lax.broadcasted_iota(I32, (LANES, tq), 0)

    def head_out(h):
        cols = slice(h * tq, (h + 1) * tq)
        return acc_ref[0:LANES, cols] / acc_ref[LANES:LANES + 1, cols]

    for s in range(4):
        o = jnp.where(frow < ATT_HEAD_DIM, head_out(s), head_out(s + 4)).T
        az = az_ref[:, s * LANES:(s + 1) * LANES].astype(F32)
        ga_ref[:, s * LANES:(s + 1) * LANES] = (o * _silu(az)).astype(ga_ref.dtype)


def _attn_prompt(aq, iq, iw, az, akb, avt, ikk, tril, nb, t, topk):
    n = aq.shape[0]
    nq = t // Q_TILE
    qrow = lambda b, i: (b * nq + i, 0)
    seq = lambda b, i: (b, 0)
    return pl.pallas_call(
        functools.partial(_attn_prompt_kernel, topk=topk),
        grid=(nb, nq),
        in_specs=[pl.BlockSpec((Q_TILE, 512), qrow),
                  pl.BlockSpec((Q_TILE, 256), qrow),
                  pl.BlockSpec((Q_TILE, LANES), qrow),
                  pl.BlockSpec((Q_TILE, 512), qrow),
                  pl.BlockSpec((t, LANES), seq),
                  pl.BlockSpec((VT_ROWS, t), lambda b, i: (0, b)),
                  pl.BlockSpec((t, LANES), seq),
                  pl.BlockSpec((LANES, LANES), lambda b, i: (0, 0))],
        out_specs=pl.BlockSpec((Q_TILE, 512), qrow),
        out_shape=jax.ShapeDtypeStruct((n, 512), aq.dtype),
        scratch_shapes=[pltpu.VMEM((t, Q_TILE), F32),
                        pltpu.VMEM((t, Q_TILE), BF16),
                        pltpu.VMEM((ATT_HEADS, 8, Q_TILE), F32),
                        pltpu.VMEM((VT_ROWS, ATT_HEADS * Q_TILE), F32)],
        compiler_params=pltpu.CompilerParams(dimension_semantics=("arbitrary", "arbitrary"),
                                             vmem_limit_bytes=VMEM_LIMIT),
        name="attn_prompt",
    )(aq, iq, iw, az, akb, avt, ikk, tril)


def _sample_scores_kernel(pt_ref, iq_ref, iw_ref, iknew_ref, *rest, n_pages, page, ts, group):
    del pt_ref
    page_refs, sc_ref = rest[:group * n_pages], rest[group * n_pages]
    past = n_pages * page
    row = lax.broadcasted_iota(I32, (ts, page), 0)
    col = lax.broadcasted_iota(I32, (ts, page), 1)
    for g in range(group):
        rows = slice(g * ts, (g + 1) * ts)
        iq = iq_ref[rows, :].astype(F32)
        heads = [iq[:, h * IDX_DIM:(h + 1) * IDX_DIM] for h in range(IDX_HEADS)]
        iqs = jnp.concatenate(heads, axis=0).astype(BF16)
        iw = iw_ref[rows, :] * (IDX_HEADS ** -0.5)
        iwb_past = [jnp.broadcast_to(iw[:, h:h + 1], (ts, past)) for h in range(IDX_HEADS)]
        iwb = [w[:, :page] for w in iwb_past]
        ikt = jnp.concatenate([r[...].astype(BF16)
                               for r in page_refs[g * n_pages:(g + 1) * n_pages]], axis=1)
        sc_ref[rows, :past] = _indexer_scores(iqs, ikt, iwb_past, ts, feature_major=True)
        new = jnp.concatenate([iknew_ref[rows, :], jnp.zeros((page - ts, IDX_DIM), F32)], axis=0)
        sc = _indexer_scores(iqs, new.astype(BF16), iwb, ts)
        sc_ref[rows, past:past + page] = jnp.where(col <= row, sc, -jnp.inf)


def _sample_scores(page_table, iq, iw, ik_new, cache_idx, layer, ts):
    nb, n_pages = page_table.shape
    page = cache_idx.shape[3]
    lp = (n_pages + 1) * page
    group = SAMPLE_GROUP if nb % SAMPLE_GROUP == 0 else 1
    row = lambda b, pt: (b, 0)
    rt = group * ts
    grid_spec = pltpu.PrefetchScalarGridSpec(
        num_scalar_prefetch=1,
        grid=(nb // group,),
        in_specs=[pl.BlockSpec((rt, 256), row),
                  pl.BlockSpec((rt, LANES), row),
                  pl.BlockSpec((rt, IDX_DIM), row)] + _page_specs((IDX_DIM, page), layer, n_pages,
                                                                  group),
        out_specs=pl.BlockSpec((rt, lp), row),
    )
    return pl.pallas_call(
        functools.partial(_sample_scores_kernel, n_pages=n_pages, page=page, ts=ts, group=group),
        grid_spec=grid_spec,
        out_shape=jax.ShapeDtypeStruct((nb * ts, lp), F32),
        compiler_params=pltpu.CompilerParams(dimension_semantics=("arbitrary",),
                                             vmem_limit_bytes=VMEM_LIMIT),
        name="sample_scores",
    )(page_table, iq, iw, ik_new, *([cache_idx] * (group * n_pages)))


def _sample_select_kernel(sc_in_ref, tril_ref, bias_ref, sc_ref, sb_ref, *, topk, lp):
    nch = lp // LANES
    for c in range(nch):
        rows = slice(c * LANES, (c + 1) * LANES)
        t = sc_in_ref[:, rows].T
        sc_ref[rows, :] = t
        sb_ref[rows, :] = t.astype(BF16)
    tau, need = _kth_largest_t(sc_ref, sb_ref, nch, LANES, topk)
    need = jnp.where(tau == -jnp.inf, 0.0, need)
    tril = tril_ref[...]
    tie_carry = jnp.zeros((8, LANES), F32)
    for c in range(nch):
        rows = slice(c * LANES, (c + 1) * LANES)
        bias, tie_carry = _select_bias_t(_rows8(sc_ref[rows, :]), tau, need, tie_carry, tril, LANES)
        bias_ref[:, rows] = bias.reshape(LANES, LANES).T


def _sample_select(scores, tril, topk):
    n, lp = scores.shape
    rows = LANES
    return pl.pallas_call(
        functools.partial(_sample_select_kernel, topk=topk, lp=lp),
        grid=(n // rows,),
        in_specs=[pl.BlockSpec((rows, lp), lambda i: (i, 0)),
                  pl.BlockSpec((LANES, LANES), lambda i: (0, 0))],
        out_specs=pl.BlockSpec((rows, lp), lambda i: (i, 0)),
        out_shape=jax.ShapeDtypeStruct((n, lp), F32),
        scratch_shapes=[pltpu.VMEM((lp, LANES), F32), pltpu.VMEM((lp, LANES), BF16)],
        compiler_params=pltpu.CompilerParams(dimension_semantics=("arbitrary",),
                                             vmem_limit_bytes=VMEM_LIMIT),
        name="sample_select",
    )(scores, tril)


def _sample_attn_kernel(pt_ref, aq_ref, az_ref, bias_ref, knew_ref, vnew_ref, *rest,
                        n_pages, page, ts, group):
    del pt_ref
    nk = group * n_pages
    k_refs, v_refs, ga_ref = rest[:nk], rest[nk:2 * nk], rest[2 * nk]
    past = n_pages * page
    pad = jnp.zeros((page - ts, LANES), F32)
    lane = lax.broadcasted_iota(I32, (ts, LANES), 1)

    for g in range(group):
        rows = slice(g * ts, (g + 1) * ts)
        aq = aq_ref[rows, :].astype(F32) * (ATT_HEAD_DIM ** -0.5)
        lo_heads, hi_heads = [], []
        for s in range(4):
            lo, hi = _split_heads(aq[:, s * LANES:(s + 1) * LANES], ts)
            lo_heads.append(lo)
            hi_heads.append(hi)
        qs = jnp.concatenate(lo_heads + hi_heads, axis=0).astype(BF16)
        k_new = jnp.concatenate([knew_ref[rows, :], pad], axis=0).astype(BF16)
        v_new = jnp.concatenate([vnew_ref[rows, :], pad], axis=0).astype(BF16)
        pages = slice(g * n_pages, (g + 1) * n_pages)
        kt = jnp.concatenate([r[...].astype(BF16) for r in k_refs[pages]], axis=1)
        vt = jnp.concatenate([r[...].astype(BF16) for r in v_refs[pages]], axis=1)
        s = jnp.concatenate([jnp.dot(qs, kt, preferred_element_type=F32),
                             lax.dot_general(qs, k_new, NT_DIMS, preferred_element_type=F32)],
                            axis=1)
        s = s + jnp.concatenate([bias_ref[rows, :]] * ATT_HEADS, axis=0)
        p = jnp.exp(s - jnp.max(s, axis=1, keepdims=True))
        denom = jnp.sum(p, axis=1, keepdims=True)
        pb = p.astype(BF16)
        acc = (lax.dot_general(pb[:, :past], vt, NT_DIMS, preferred_element_type=F32)
               + jnp.dot(pb[:, past:], v_new, preferred_element_type=F32))
        o = acc / denom
        for s in range(4):
            oo = jnp.where(lane < ATT_HEAD_DIM, o[s * ts:(s + 1) * ts], o[(s + 4) * ts:(s + 5) * ts])
            az = az_ref[rows, s * LANES:(s + 1) * LANES].astype(F32)
            ga_ref[rows, s * LANES:(s + 1) * LANES] = (oo * _silu(az)).astype(ga_ref.dtype)


def _page_specs(shape, layer, n_pages, group):
    return [pl.BlockSpec((None, None) + shape,
                         functools.partial(lambda b, pt, g, p: (layer, pt[b * group + g, p], 0, 0),
                                           g=g, p=p))
            for g in range(group) for p in range(n_pages)]


def _sample_attn(page_table, aq, az, bias, k_new, v_new, cache_k, cache_v, layer, ts):
    nb, n_pages = page_table.shape
    page = cache_k.shape[3]
    lp = (n_pages + 1) * page
    group = SAMPLE_GROUP if nb % SAMPLE_GROUP == 0 else 1
    row = lambda b, pt: (b, 0)
    page_specs = _page_specs((LANES, page), layer, n_pages, group)
    rt = group * ts
    grid_spec = pltpu.PrefetchScalarGridSpec(
        num_scalar_prefetch=1,
        grid=(nb // group,),
        in_specs=[pl.BlockSpec((rt, 512), row),
                  pl.BlockSpec((rt, 512), row),
                  pl.BlockSpec((rt, lp), row),
                  pl.BlockSpec((rt, LANES), row),
                  pl.BlockSpec((rt, LANES), row)] + page_specs + page_specs,
        out_specs=pl.BlockSpec((rt, 512), row),
    )
    return pl.pallas_call(
        functools.partial(_sample_attn_kernel, n_pages=n_pages, page=page, ts=ts, group=group),
        grid_spec=grid_spec,
        out_shape=jax.ShapeDtypeStruct((nb * ts, 512), aq.dtype),
        compiler_params=pltpu.CompilerParams(dimension_semantics=("arbitrary",),
                                             vmem_limit_bytes=VMEM_LIMIT),
        name="sample_attn",
    )(page_table, aq, az, bias, k_new, v_new, *([cache_k] * (group * n_pages)),
      *([cache_v] * (group * n_pages)))


def _merge_kernel(x_ref, p_ref, gr_ref, ga_ref, br_ref, ba_ref, wor_ref, woa_ref, wout_ref,
                  wpg_ref, wpp_ref, y_ref):
    u_r = jnp.dot(br_ref[...].astype(BF16), wor_ref[...], preferred_element_type=F32)
    u_a = jnp.dot(ba_ref[...].astype(BF16), woa_ref[...], preferred_element_type=F32)
    m = (jax.nn.sigmoid(gr_ref[...].astype(F32)) * u_r
         + jax.nn.sigmoid(ga_ref[...].astype(F32)) * u_a)
    x1 = x_ref[...] + jnp.dot(m.astype(BF16), wout_ref[...], preferred_element_type=F32)
    gate = jax.nn.sigmoid(jnp.dot(x1.astype(BF16), wpg_ref[...], preferred_element_type=F32))
    y_ref[...] = x1 + gate * jnp.dot(p_ref[...].astype(BF16), wpp_ref[...],
                                     preferred_element_type=F32)


def _merge(x, p_all, layer, gr, ga, br, ba, weights, tm):
    n, d = x.shape
    row = lambda i: (i, 0)
    wspec = lambda w: pl.BlockSpec((None,) + w.shape[1:], lambda i: (layer, 0, 0))
    return pl.pallas_call(
        _merge_kernel,
        grid=(n // tm,),
        in_specs=[pl.BlockSpec((tm, d), row),
                  pl.BlockSpec((None, tm, p_all.shape[2]), lambda i: (layer, i, 0)),
                  pl.BlockSpec((tm, d), row),
                  pl.BlockSpec((tm, d), row),
                  pl.BlockSpec((tm, 512), row),
                  pl.BlockSpec((tm, 512), row)] + [wspec(w) for w in weights],
        out_specs=pl.BlockSpec((tm, d), row),
        out_shape=jax.ShapeDtypeStruct((n, d), F32),
        compiler_params=pltpu.CompilerParams(dimension_semantics=("arbitrary",),
                                             vmem_limit_bytes=VMEM_LIMIT),
        name="merge",
    )(x, p_all, gr, ga, br, ba, *weights)


def _pair_heads_rows(m):
    d = m.shape[1]
    m = m.reshape(ATT_KV_HEADS, ATT_HEADS // ATT_KV_HEADS, ATT_HEAD_DIM, d)
    return jnp.concatenate([m[0], m[1]], axis=1).reshape(ATT_HEADS * ATT_HEAD_DIM, d)


def _pack_w_in(wt):
    sizes = (512, 512, 512, 512, 512, 128, 128, 512, 256, 64, 4, 1024, 1024)
    offs, o = [], 0
    for s in sizes:
        offs.append((o, o + s))
        o += s
    rq, rk, rv, rz, aq, ak, av, az, iq, ik, iw, gr, ga = [wt[a:b] for a, b in offs]
    iwp = jnp.pad(iw, ((0, LANES - IDX_HEADS), (0, 0)))
    packed = jnp.concatenate([rq, rk, rv, rz, _pair_heads_rows(aq), _pair_heads_rows(az), ak, av,
                              iq, ik, ik, iwp, gr, ga], axis=0)
    return packed.astype(BF16)


def _rope_tables(pos):
    pf = pos.astype(F32)[:, None]
    half = RET_DK // 2
    freqs = jnp.exp(-math.log(RET_THETA) * jnp.arange(half, dtype=F32) / half)
    ang = pf * freqs[None, :]
    cos, sin = jnp.cos(ang), jnp.sin(ang)
    rc = jnp.concatenate([cos, cos], axis=1)
    rs = jnp.concatenate([-sin, sin], axis=1)
    half = ROPE_DIM // 2
    freqs = jnp.exp(-math.log(ROPE_THETA) * jnp.arange(half, dtype=F32) / half)
    ang = pf * freqs[None, :]
    cos, sin = jnp.cos(ang), jnp.sin(ang)
    r = pos.shape[0]
    rest = ATT_HEAD_DIM - ROPE_DIM
    one, zero, zh = jnp.ones((r, rest), F32), jnp.zeros((r, rest), F32), jnp.zeros((r, half), F32)
    ac = jnp.concatenate([cos, cos, one], axis=1)
    as1 = jnp.concatenate([-sin, zh, zero], axis=1)
    as2 = jnp.concatenate([zh, sin, zero], axis=1)
    tile2 = lambda a: jnp.concatenate([a, a], axis=1)
    return rc, rs, tile2(ac), tile2(as1), tile2(as2)


def _decay_tables(c_eff, cpad):
    h = RET_HEADS
    log_g = jnp.log1p(-jnp.exp2(-5.0 - jnp.arange(h, dtype=F32)))
    c = jnp.arange(cpad, dtype=F32)
    diff = c[:, None] - c[None, :]
    di = jnp.where(diff[None] >= 0, jnp.exp(jnp.maximum(diff, 0.0)[None] * log_g[:, None, None]), 0.0)
    dq = jnp.exp((c[:, None] + 1.0) * log_g[None, :])
    dk = jnp.exp((c_eff - 1.0 - c)[:, None] * log_g[None, :])
    dk = jnp.where(c[:, None] < c_eff, dk, 0.0)
    ds = jnp.exp(c_eff * log_g)
    rep = lambda a: jnp.repeat(a, RET_DK, axis=-1)
    return di, rep(dq), rep(dk), rep(ds[None, :])


def _tril(n):
    r = lax.broadcasted_iota(I32, (n, n), 0)
    c = lax.broadcasted_iota(I32, (n, n), 1)
    return jnp.where(c <= r, 1.0, 0.0).astype(BF16)


def _group_sum_matrix():
    r = lax.broadcasted_iota(I32, (LANES, LANES), 0) // ATT_HEAD_DIM
    c = lax.broadcasted_iota(I32, (LANES, LANES), 1) // ATT_HEAD_DIM
    return jnp.where(r == c, 1.0, 0.0).astype(BF16)


def _layer_weights(i, norm_gain, w_in_t, q_norm_gain, k_norm_gain):
    return dict(
        gain=norm_gain[i][None, :],
        w_in=_pack_w_in(w_in_t[:, i, :]),
        qg=jnp.tile(q_norm_gain[i], 2)[None, :],
        kg=jnp.tile(k_norm_gain[i], 2)[None, :],
    )


def _merge_weights(w_o_ret, w_o_att, w_out, w_ple_gate, w_ple_proj):
    depth, _, d = w_o_att.shape
    woa = w_o_att.reshape(depth, ATT_KV_HEADS, ATT_HEADS // ATT_KV_HEADS, ATT_HEAD_DIM, d)
    woa = jnp.concatenate([woa[:, 0], woa[:, 1]], axis=2).reshape(depth, -1, d)
    return tuple(w.astype(BF16) for w in (w_o_ret, woa, w_out, w_ple_gate, w_ple_proj))


def kernel(x_prompt, x_sample, cache_k, cache_v, cache_idx_k, state_ret, page_table, p_prompt,
           p_sample, norm_gain, w_in, q_norm_gain, k_norm_gain, w_o_ret, w_o_att, w_out, w_ple_gate,
           w_ple_proj):
    bp, tp, d = x_prompt.shape
    bs, ts, _ = x_sample.shape
    depth = w_in.shape[0]
    n_pool, page = cache_k.shape[1], cache_k.shape[2]
    n_pages = page_table.shape[1]
    past = n_pages * page
    topk_p = min(TOPK_MAX, tp // 4)
    topk_s = min(TOPK_MAX, (past + ts) // 4)
    assert tp % KEY_CHUNK == 0 and tp % RET_BLOCK == 0 and ts <= page and ts % 8 == 0
    assert topk_p <= KEY_CHUNK

    np_, ns_ = bp * tp, bs * ts
    tm_p = 512 if np_ % 512 == 0 else Q_TILE
    tm_s = 256 if ns_ % 256 == 0 else ns_
    assert tm_s % ts == 0 and tp % tm_p == 0

    gsum = _group_sum_matrix()
    tabs_p = _rope_tables(jnp.arange(tp, dtype=I32))
    tabs_s = _rope_tables(past + (jnp.arange(tm_s, dtype=I32) % ts))
    decay_p = _decay_tables(float(RET_BLOCK), RET_BLOCK)
    decay_s = _decay_tables(float(ts), DEC_PAD)
    tril_p = _tril(LANES)
    ck = cache_k.reshape(depth, n_pool, page, LANES).transpose(0, 1, 3, 2)
    cv = cache_v.reshape(depth, n_pool, page, LANES).transpose(0, 1, 3, 2)
    cik = cache_idx_k.transpose(0, 1, 3, 2)
    w_in_t = w_in.transpose(2, 0, 1)
    s0_p = jnp.zeros((1, bp, RET_HEADS, RET_DK, RET_DV), F32)

    xp = x_prompt.reshape(np_, d)
    xs = x_sample.reshape(ns_, d)
    outs = {k: [] for k in ("kp", "vp", "ikp", "sp", "ks", "vs", "iks", "ss")}

    pp_all = p_prompt.reshape(depth, np_, -1)
    ps_all = p_sample.reshape(depth, ns_, -1)
    mw = _merge_weights(w_o_ret, w_o_att, w_out, w_ple_gate, w_ple_proj)

    for i in range(depth):
        lw = _layer_weights(i, norm_gain, w_in_t, q_norm_gain, k_norm_gain)

        (rq, rk, rv, rz, aq, az, akb, avt, iq, ikk, gr, ga, ak, av, ik, iw) = _inproj(
            xp, lw["gain"], lw["w_in"], tabs_p, lw["qg"], lw["kg"], gsum, tm_p, tp // tm_p, BF16)
        seq3 = lambda a: a.reshape(bp, tp, 512)
        b_r, s_new = _retention(seq3(rq), seq3(rk), seq3(rv), seq3(rz), s0_p, 0, decay_p, bp,
                                tp // RET_BLOCK, RET_BLOCK, BF16)
        b_r = b_r.reshape(np_, 512)
        b_a = _attn_prompt(aq, iq, iw, az, akb, avt, ikk, tril_p, bp, tp, topk_p)
        xp = _merge(xp, pp_all, i, gr, ga, b_r, b_a, mw, tm_p)
        outs["kp"].append(ak.reshape(bp, tp, ATT_KV_HEADS, ATT_HEAD_DIM))
        outs["vp"].append(av.reshape(bp, tp, ATT_KV_HEADS, ATT_HEAD_DIM))
        outs["ikp"].append(ik.reshape(bp, tp, IDX_DIM))
        outs["sp"].append(s_new)

        (rq, rk, rv, rz, aq, az, akb, avb, iq, ikk, gr, ga, ak, av, ik, iw) = _inproj(
            xs, lw["gain"], lw["w_in"], tabs_s, lw["qg"], lw["kg"], gsum, tm_s, 1, F32)
        tok3 = lambda a: a.reshape(bs, ts, 512)
        b_r, s_new = _retention(tok3(rq), tok3(rk), tok3(rv), tok3(rz), state_ret, i, decay_s, bs,
                                1, ts, F32)
        b_r = b_r.reshape(ns_, 512)
        keys = _sample_scores(page_table, iq, iw, ik, cik, i, ts)
        bias = _sample_select(keys, tril_p, topk_s)
        b_a = _sample_attn(page_table, aq, az, bias, ak, av, ck, cv, i, ts)
        xs = _merge(xs, ps_all, i, gr, ga, b_r, b_a, mw, tm_s)
        outs["ks"].append(ak.reshape(bs, ts, ATT_KV_HEADS, ATT_HEAD_DIM))
        outs["vs"].append(av.reshape(bs, ts, ATT_KV_HEADS, ATT_HEAD_DIM))
        outs["iks"].append(ik.reshape(bs, ts, IDX_DIM))
        outs["ss"].append(s_new)

    st = lambda k: jnp.stack(outs[k])
    return (xp.reshape(bp, tp, d), xs.reshape(bs, ts, d), st("kp"), st("vp"), st("ikp"), st("sp"),
            st("ks"), st("vs"), st("iks"), st("ss"))
```

```python
import functools
import math

import jax
import jax.numpy as jnp
from jax import lax
from jax.experimental import pallas as pl
from jax.experimental.pallas import tpu as pltpu

F32 = jnp.float32
BF16 = jnp.bfloat16
I32 = jnp.int32

RET_HEADS = 4
RET_DK = 128
RET_DV = 128
RET_CHUNK = 128
RET_BLOCK = 256
SAMPLE_GROUP = 4
DEC_PAD = 32
RET_THETA = 10000.0
ATT_HEADS = 8
ATT_KV_HEADS = 2
ATT_HEAD_DIM = 64
ROPE_THETA = 500000.0
ROPE_DIM = ATT_HEAD_DIM // 4
IDX_HEADS = 4
IDX_DIM = 64
TOPK_MAX = 256
NORM_EPS = 1e-6
GN_EPS = 1e-5

LANES = 128
Q_TILE = 128
KEY_CHUNK = 512
VMEM_LIMIT = 56 * 1024 * 1024
NEG_BIG = -1e30

C_RQ, C_RK, C_RV, C_RZ = 0, 512, 1024, 1536
C_AQ, C_AZ, C_AK, C_AV = 2048, 2560, 3072, 3200
C_IQ, C_IKK, C_IW, C_GR, C_GA = 3328, 3584, 3712, 3840, 4864
W_PACKED = 5888

NT_DIMS = (((1,), (1,)), ((), ()))
VT_ROWS = LANES + 16
LOG2E = math.log2(math.e)


def _silu(x):
    return x * jax.nn.sigmoid(x)


def _inproj_kernel(x_ref, g_ref, w_ref, rc_ref, rs_ref, ac_ref, as1_ref, as2_ref, qg_ref, kg_ref,
                   gsum_ref,
                   rq_o, rk_o, rv_o, rz_o, aq_o, az_o, akb_o, avt_o, iq_o, ikk_o, gr_o, ga_o,
                   ak_o, av_o, ik_o, iw_o):
    sd = rq_o.dtype
    x = x_ref[...]
    ms = jnp.mean(x * x, axis=-1, keepdims=True)
    hb = ((x * lax.rsqrt(ms + NORM_EPS)) * g_ref[...]).astype(BF16)

    def mm(c0, width):
        return lax.dot_general(hb, w_ref[c0:c0 + width, :], NT_DIMS, preferred_element_type=F32)

    rc, rs = rc_ref[...], rs_ref[...]
    ac, as1, as2 = ac_ref[...], as1_ref[...], as2_ref[...]
    gsum = gsum_ref[...]

    def rope_ret(z):
        return z * rc + pltpu.roll(z, 64, 1) * rs

    def rope_att(z):
        return z * ac + pltpu.roll(z, LANES - 8, 1) * as1 + pltpu.roll(z, 8, 1) * as2

    def head_norm(z, gain):
        sq = z * z
        hi = sq.astype(BF16)
        lo = (sq - hi.astype(F32)).astype(BF16)
        ssq = (jnp.dot(hi, gsum, preferred_element_type=F32)
               + jnp.dot(lo, gsum, preferred_element_type=F32))
        return (z * lax.rsqrt(ssq * (1.0 / ATT_HEAD_DIM) + NORM_EPS)) * gain

    def slab(z, s):
        return z[:, s * LANES:(s + 1) * LANES]

    z = mm(C_RQ, 512)
    for s in range(4):
        rq_o[:, s * LANES:(s + 1) * LANES] = rope_ret(slab(z, s)).astype(sd)
    z = mm(C_RK, 512)
    for s in range(4):
        rk_o[:, s * LANES:(s + 1) * LANES] = (rope_ret(slab(z, s)) * (RET_DK ** -0.5)).astype(sd)
    rv_o[...] = mm(C_RV, 512).astype(sd)
    rz_o[...] = mm(C_RZ, 512).astype(sd)

    qg, kg = qg_ref[...], kg_ref[...]
    z = mm(C_AQ, 512)
    for s in range(4):
        aq_o[:, s * LANES:(s + 1) * LANES] = rope_att(head_norm(slab(z, s), qg)).astype(sd)
    az_o[...] = mm(C_AZ, 512).astype(sd)

    z = mm(C_AK, 256)
    k = rope_att(head_norm(slab(z, 0), kg))
    ak_o[...] = k
    akb_o[...] = k.astype(BF16)
    v = slab(z, 1)
    av_o[...] = v
    avt_o[...] = jnp.concatenate([v.T, jnp.ones((VT_ROWS - LANES, v.shape[0]), F32)],
                                 axis=0).astype(BF16)

    z = mm(C_IQ, 512)
    for s in range(2):
        iq_o[:, s * LANES:(s + 1) * LANES] = rope_att(slab(z, s)).astype(sd)
    ikk = rope_att(slab(z, 2))
    ikk_o[...] = ikk.astype(BF16)
    ik_o[...] = ikk[:, :IDX_DIM]
    iw_o[...] = slab(z, 3)

    gr_o[...] = mm(C_GR, 1024).astype(sd)
    ga_o[...] = mm(C_GA, 1024).astype(sd)


def _inproj(x, gain, w_packed, tabs, qg, kg, gsum, tm, pos_period_tiles, sd):
    n, d = x.shape
    grid = (n // tm,)
    row = lambda i: (i, 0)
    const = lambda i: (0, 0)
    tab = lambda i: (i % pos_period_tiles, 0)
    in_specs = [
        pl.BlockSpec((tm, d), row),
        pl.BlockSpec((1, d), const),
        pl.BlockSpec((W_PACKED, d), const),
    ] + [pl.BlockSpec((tm, LANES), tab)] * 5 + [
        pl.BlockSpec((1, LANES), const),
        pl.BlockSpec((1, LANES), const),
        pl.BlockSpec((LANES, LANES), const),
    ]
    widths = [(512, sd)] * 6 + [(128, BF16), (128, BF16), (256, sd), (128, BF16), (1024, sd), (1024, sd),
                                (128, F32), (128, F32), (IDX_DIM, F32), (128, F32)]
    out_shape = [jax.ShapeDtypeStruct((n, w), dt) for w, dt in widths]
    out_specs = [pl.BlockSpec((tm, w), row) for w, _ in widths]
    out_shape[7] = jax.ShapeDtypeStruct((VT_ROWS, n), BF16)
    out_specs[7] = pl.BlockSpec((VT_ROWS, tm), lambda i: (0, i))
    return pl.pallas_call(
        _inproj_kernel,
        grid=grid,
        in_specs=in_specs,
        out_specs=out_specs,
        out_shape=out_shape,
        compiler_params=pltpu.CompilerParams(dimension_semantics=("arbitrary",),
                                             vmem_limit_bytes=VMEM_LIMIT),
        name="inproj",
    )(x, gain, w_packed, *tabs, qg, kg, gsum)


def _retention_kernel(q_ref, k_ref, v_ref, z_ref, s0_ref, di_ref, dq_ref, dk_ref, ds_ref,
                      o_ref, sout_ref, s_scr, *, group):
    c = pl.program_id(1)
    nc = pl.num_programs(1)

    @pl.when(c == 0)
    def _():
        s_scr[...] = s0_ref[...]

    rows = q_ref.shape[1]
    cpad = di_ref.shape[1]

    def load(ref, g, hs):
        x = ref[g, :, hs].astype(F32)
        if rows < cpad:
            x = jnp.concatenate([x, jnp.zeros((cpad - rows, LANES), F32)], axis=0)
        return x

    for g in range(group):
        for h in range(RET_HEADS):
            hs = slice(h * LANES, (h + 1) * LANES)
            q = load(q_ref, g, hs)
            k = load(k_ref, g, hs)
            v = load(v_ref, g, hs).astype(BF16)
            s_old = s_scr[g, h]
            inner = lax.dot_general(q.astype(BF16), k.astype(BF16), NT_DIMS,
                                    preferred_element_type=F32) * di_ref[h]
            o = (jnp.dot(inner.astype(BF16), v, preferred_element_type=F32)
                 + jnp.dot((q * dq_ref[:, hs]).astype(BF16), s_old.astype(BF16),
                           preferred_element_type=F32))
            kd = (k * dk_ref[:, hs]).T.astype(BF16)
            s_scr[g, h] = s_old * ds_ref[:, hs] + jnp.dot(kd, v, preferred_element_type=F32)
            mu = jnp.mean(o, axis=-1, keepdims=True)
            cen = o - mu
            var = jnp.mean(cen * cen, axis=-1, keepdims=True)
            gn = cen * lax.rsqrt(var + GN_EPS)
            o_ref[g, :, hs] = (gn[:rows] * _silu(z_ref[g, :, hs].astype(F32))).astype(o_ref.dtype)

    @pl.when(c == nc - 1)
    def _():
        sout_ref[...] = s_scr[...]


def _retention(rq, rk, rv, rz, s0_all, layer, decay, nb, nchunks, rows, out_dtype):
    di, dq, dk, ds = decay
    c = di.shape[1]
    group = max(g for g in (8, 4, 2, 1) if nb % g == 0)
    blk = pl.BlockSpec((group, rows, 512), lambda b, j: (b, j, 0))
    st_in = pl.BlockSpec((None, group, RET_HEADS, RET_DK, RET_DV), lambda b, j: (layer, b, 0, 0, 0))
    st = pl.BlockSpec((group, RET_HEADS, RET_DK, RET_DV), lambda b, j: (b, 0, 0, 0))
    const2 = lambda b, j: (0, 0)
    return pl.pallas_call(
        functools.partial(_retention_kernel, group=group),
        grid=(nb // group, nchunks),
        in_specs=[blk, blk, blk, blk, st_in,
                  pl.BlockSpec((RET_HEADS, c, c), lambda b, j: (0, 0, 0)),
                  pl.BlockSpec((c, 512), const2),
                  pl.BlockSpec((c, 512), const2),
                  pl.BlockSpec((1, 512), const2)],
        out_specs=[blk, st],
        out_shape=[jax.ShapeDtypeStruct((nb, nchunks * rows, 512), out_dtype),
                   jax.ShapeDtypeStruct((nb, RET_HEADS, RET_DK, RET_DV), F32)],
        scratch_shapes=[pltpu.VMEM((group, RET_HEADS, RET_DK, RET_DV), F32)],
        compiler_params=pltpu.CompilerParams(dimension_semantics=("arbitrary", "arbitrary"),
                                             vmem_limit_bytes=VMEM_LIMIT),
        name="retention",
    )(rq, rk, rv, rz, s0_all, di, dq, dk, ds)


KEY_NEG_INF = -2139095041


def _cand_float(key):
    bits = key ^ (lax.shift_right_arithmetic(key, 31) & 0x7FFFFFFF)
    return jnp.where(key < KEY_NEG_INF, -jnp.inf, pltpu.bitcast(bits, F32))


def _split_heads(x2, rows):
    lane = lax.broadcasted_iota(I32, (rows, LANES), 1)
    lo = lane < ATT_HEAD_DIM
    return jnp.where(lo, x2, 0.0), jnp.where(lo, 0.0, x2)


def _indexer_scores(iqs, ikc, iwb, rows, feature_major=False):
    if feature_major:
        lg = jnp.dot(iqs, ikc, preferred_element_type=F32)
    else:
        lg = lax.dot_general(iqs, ikc, NT_DIMS, preferred_element_type=F32)
    sc = None
    for h in range(IDX_HEADS):
        t = jnp.maximum(lg[h * rows:(h + 1) * rows] * (IDX_DIM ** -0.5), 0.0) * iwb[h]
        sc = t if sc is None else sc + t
    return sc


def _rows8(x):
    return x.reshape(x.shape[0] // 8, 8, LANES)


def _fold8(x8, op):
    return jnp.broadcast_to(op(x8, axis=0, keepdims=True), (8, LANES))


def _reduce0(x3, op, ways=8):
    accs = [x3[j] for j in range(ways)]
    for j in range(ways, x3.shape[0]):
        accs[j % ways] = op(accs[j % ways], x3[j])
    while len(accs) > 1:
        accs = [op(accs[a], accs[a + 1]) for a in range(0, len(accs), 2)]
    return accs[0]


I16 = jnp.int16
I16_MIN = -(2 ** 15)


def _chunk_loop(nchunks, body, init):
    if isinstance(nchunks, int):
        for c in range(nchunks):
            init = body(c, init)
        return init
    return lax.fori_loop(0, nchunks, body, init)


def _chunk_start(c, kc):
    return c * kc if isinstance(c, int) else pl.multiple_of(c * kc, kc)


def _count_f(sc_ref, nchunks, kc, cand, strict, ways=4):
    def body(c, accs):
        x = _rows8(sc_ref[pl.ds(_chunk_start(c, kc), kc), :])
        ones = jnp.where((x > cand[None]) if strict else (x >= cand[None]), 1.0, 0.0)
        accs = list(accs)
        for j in range(kc // 8):
            accs[j % ways] = accs[j % ways] + ones[j]
        return tuple(accs)

    accs = _chunk_loop(nchunks, body, tuple(jnp.zeros((8, LANES), F32) for _ in range(ways)))
    return _fold8((accs[0] + accs[1]) + (accs[2] + accs[3]), jnp.sum)


def _count_b(sb_ref, nchunks, kc, cand, ways=4):
    c16 = jnp.concatenate([cand, cand], axis=0).astype(BF16)[None]

    def body(c, accs):
        x = sb_ref[pl.ds(_chunk_start(c, kc), kc), :].reshape(kc // 16, 16, LANES)
        ones = jnp.where(x >= c16, jnp.int16(1), jnp.int16(0))
        accs = list(accs)
        for j in range(kc // 16):
            accs[j % ways] = accs[j % ways] + ones[j]
        return tuple(accs)

    accs = _chunk_loop(nchunks, body, tuple(jnp.zeros((16, LANES), I16) for _ in range(ways)))
    tot = ((accs[0] + accs[1]) + (accs[2] + accs[3])).astype(I32).astype(F32)
    return jnp.broadcast_to(jnp.sum(tot, axis=0, keepdims=True), (8, LANES))


def _kth_largest_t(sc_ref, sb_ref, nchunks, kc, topk):
    kf = float(topk)
    zero = jnp.zeros((8, LANES), I32)
    c0 = _count_b(sb_ref, nchunks, kc, _cand_float(zero))
    h = jnp.where(c0 >= kf, 0, I16_MIN).astype(I32)

    def hi_step(b, h):
        cand = h | lax.shift_left(jnp.int32(1), 14 - b)
        cnt = _count_b(sb_ref, nchunks, kc, _cand_float(cand * 65536))
        return jnp.where(cnt >= kf, cand, h)

    h = lax.fori_loop(0, 15, hi_step, h)
    base = jnp.maximum(h, I16_MIN + 1) * 65536 - 32768

    def lo_step(b, o):
        cand = o | lax.shift_left(jnp.int32(1), 16 - b)
        cnt = _count_f(sc_ref, nchunks, kc, _cand_float(base + cand), False)
        return jnp.where(cnt >= kf, cand, o)

    tau = _cand_float(base + lax.fori_loop(0, 17, lo_step, zero))
    return tau, kf - _count_f(sc_ref, nchunks, kc, tau, True)


def _select_bias_t(kk3, tau, need, tie_carry, tril, kc):
    eqf = jnp.where(kk3 == tau[None], 1.0, 0.0).reshape(kc, LANES).astype(BF16)
    ranks = []
    for blk in range(kc // LANES):
        r = jnp.dot(tril, eqf[blk * LANES:(blk + 1) * LANES], preferred_element_type=F32)
        ranks.append(_rows8(r) + tie_carry[None])
        tie_carry = tie_carry + jnp.broadcast_to(r[LANES - 1:LANES, :], (8, LANES))
    rank = jnp.concatenate(ranks, axis=0)
    tie_ok = jnp.where(rank <= need[None], 0.0, NEG_BIG)
    bias = jnp.where(kk3 > tau[None], 0.0, jnp.where(kk3 == tau[None], tie_ok, NEG_BIG))
    return bias, tie_carry


def _attn_prompt_kernel(aq_ref, iq_ref, iw_ref, az_ref, kb_ref, vt_ref, ik_ref, tril_ref, ga_ref,
                        sc_ref, sb_ref, m_ref, acc_ref, *, topk):
    tq, kc = Q_TILE, KEY_CHUNK
    i = pl.program_id(1)
    nkc = lax.div(i * tq + tq + kc - 1, kc)
    qpos = i * tq + lax.broadcasted_iota(I32, (kc, tq), 1)
    krow = lax.broadcasted_iota(I32, (kc, tq), 0)

    iq = iq_ref[...].astype(F32) * (IDX_DIM ** -0.5)
    parts = []
    for s in range(2):
        parts += list(_split_heads(iq[:, s * LANES:(s + 1) * LANES], tq))
    iqs = jnp.concatenate(parts, axis=0).astype(BF16)
    iwt = (iw_ref[...] * (IDX_HEADS ** -0.5)).T

    def scores(c, carry):
        k0 = _chunk_start(c, kc)
        ikc = ik_ref[pl.ds(k0, kc), :]
        sc = None
        for pr in range(IDX_HEADS // 2):
            lg = lax.dot_general(ikc, iqs[pr * 2 * tq:(pr + 1) * 2 * tq], NT_DIMS,
                                 preferred_element_type=F32)
            for hh in range(2):
                h = 2 * pr + hh
                t = jnp.maximum(lg[:, hh * tq:(hh + 1) * tq], 0.0) * iwt[h:h + 1, :]
                sc = t if sc is None else sc + t
        sc = jnp.where(k0 + krow <= qpos, sc, -jnp.inf)
        sc_ref[pl.ds(k0, kc), :] = sc
        sb_ref[pl.ds(k0, kc), :] = sc.astype(BF16)
        return carry

    aq = aq_ref[...].astype(F32) * (ATT_HEAD_DIM ** -0.5 * LOG2E)
    lo_heads, hi_heads = [], []
    for s in range(4):
        lo, hi = _split_heads(aq[:, s * LANES:(s + 1) * LANES], tq)
        lo_heads.append(lo)
        hi_heads.append(hi)
    qs = jnp.concatenate(lo_heads + hi_heads, axis=0).astype(BF16)

    m_ref[...] = jnp.full(m_ref.shape, NEG_BIG, F32)
    acc_ref[...] = jnp.zeros(acc_ref.shape, F32)
    tril = tril_ref[...]

    def attend(tau, need, c, tie_carry):
        k0 = _chunk_start(c, kc)
        bias, tie_carry = _select_bias_t(_rows8(sc_ref[pl.ds(k0, kc), :]), tau, need, tie_carry,
                                         tril, kc)
        kcb = kb_ref[pl.ds(k0, kc), :]
        p_cols, alphas = [], []
        for pr in range(ATT_HEADS // 2):
            s = lax.dot_general(kcb, qs[pr * 2 * tq:(pr + 1) * 2 * tq], NT_DIMS,
                                preferred_element_type=F32)
            for hh in range(2):
                h = 2 * pr + hh
                sh = _rows8(s[:, hh * tq:(hh + 1) * tq]) + bias
                m_old = m_ref[h]
                m_new = jnp.maximum(m_old, _fold8(_reduce0(sh, jnp.maximum), jnp.max))
                alpha = jnp.exp2(m_old - m_new)
                p = jnp.exp2(sh - m_new[None])
                m_ref[h] = m_new
                p_cols.append(p.reshape(kc, tq).astype(BF16))
                alphas.append(alpha[0:1, :])
        pv = jnp.dot(vt_ref[:, pl.ds(k0, kc)], jnp.concatenate(p_cols, axis=1),
                     preferred_element_type=F32)
        acc_ref[...] = jnp.concatenate(alphas, axis=1) * acc_ref[...] + pv
        return tie_carry

    def select(n):
        _chunk_loop(n, scores, 0)
        return _kth_largest_t(sc_ref, sb_ref, n, kc, topk)

    tau, need = lax.switch(nkc - 1, [functools.partial(select, n)
                                     for n in range(1, sc_ref.shape[0] // kc + 1)])
    need = jnp.where(tau == -jnp.inf, 0.0, need)
    sweep = functools.partial(attend, tau, need)

    def sweep2(j, carry):
        return sweep(2 * j + 1, sweep(2 * j, carry))

    carry = lax.fori_loop(0, lax.div(nkc, 2), sweep2, jnp.zeros((8, LANES), F32))

    @pl.when(lax.rem(nkc, 2) == 1)
    def _():
        sweep(nkc - 1, carry)

    frow = lax.broadcasted_iota(I32, (LANES, tq), 0)

    def head_out(h):
        cols = slice(h * tq, (h + 1) * tq)
        return acc_ref[0:LANES, cols] / acc_ref[LANES:LANES + 1, cols]

    for s in range(4):
        o = jnp.where(frow < ATT_HEAD_DIM, head_out(s), head_out(s + 4)).T
        az = az_ref[:, s * LANES:(s + 1) * LANES].astype(F32)
        ga_ref[:, s * LANES:(s + 1) * LANES] = (o * _silu(az)).astype(ga_ref.dtype)


def _attn_prompt(aq, iq, iw, az, akb, avt, ikk, tril, nb, t, topk):
    n = aq.shape[0]
    nq = t // Q_TILE
    qrow = lambda b, i: (b * nq + i, 0)
    seq = lambda b, i: (b, 0)
    return pl.pallas_call(
        functools.partial(_attn_prompt_kernel, topk=topk),
        grid=(nb, nq),
        in_specs=[pl.BlockSpec((Q_TILE, 512), qrow),
                  pl.BlockSpec((Q_TILE, 256), qrow),
                  pl.BlockSpec((Q_TILE, LANES), qrow),
                  pl.BlockSpec((Q_TILE, 512), qrow),
                  pl.BlockSpec((t, LANES), seq),
                  pl.BlockSpec((VT_ROWS, t), lambda b, i: (0, b)),
                  pl.BlockSpec((t, LANES), seq),
                  pl.BlockSpec((LANES, LANES), lambda b, i: (0, 0))],
        out_specs=pl.BlockSpec((Q_TILE, 512), qrow),
        out_shape=jax.ShapeDtypeStruct((n, 512), aq.dtype),
        scratch_shapes=[pltpu.VMEM((t, Q_TILE), F32),
                        pltpu.VMEM((t, Q_TILE), BF16),
                        pltpu.VMEM((ATT_HEADS, 8, Q_TILE), F32),
                        pltpu.VMEM((VT_ROWS, ATT_HEADS * Q_TILE), F32)],
        compiler_params=pltpu.CompilerParams(dimension_semantics=("arbitrary", "arbitrary"),
                                             vmem_limit_bytes=VMEM_LIMIT),
        name="attn_prompt",
    )(aq, iq, iw, az, akb, avt, ikk, tril)


def _sample_scores_kernel(pt_ref, iq_ref, iw_ref, iknew_ref, *rest, n_pages, page, ts, group):
    del pt_ref
    page_refs, sc_ref = rest[:group * n_pages], rest[group * n_pages]
    past = n_pages * page
    row = lax.broadcasted_iota(I32, (ts, page), 0)
    col = lax.broadcasted_iota(I32, (ts, page), 1)
    for g in range(group):
        rows = slice(g * ts, (g + 1) * ts)
        iq = iq_ref[rows, :].astype(F32)
        heads = [iq[:, h * IDX_DIM:(h + 1) * IDX_DIM] for h in range(IDX_HEADS)]
        iqs = jnp.concatenate(heads, axis=0).astype(BF16)
        iw = iw_ref[rows, :] * (IDX_HEADS ** -0.5)
        iwb_past = [jnp.broadcast_to(iw[:, h:h + 1], (ts, past)) for h in range(IDX_HEADS)]
        iwb = [w[:, :page] for w in iwb_past]
        ikt = jnp.concatenate([r[...].astype(BF16)
                               for r in page_refs[g * n_pages:(g + 1) * n_pages]], axis=1)
        sc_ref[rows, :past] = _indexer_scores(iqs, ikt, iwb_past, ts, feature_major=True)
        new = jnp.concatenate([iknew_ref[rows, :], jnp.zeros((page - ts, IDX_DIM), F32)], axis=0)
        sc = _indexer_scores(iqs, new.astype(BF16), iwb, ts)
        sc_ref[rows, past:past + page] = jnp.where(col <= row, sc, -jnp.inf)


def _sample_scores(page_table, iq, iw, ik_new, cache_idx, layer, ts):
    nb, n_pages = page_table.shape
    page = cache_idx.shape[3]
    lp = (n_pages + 1) * page
    group = SAMPLE_GROUP if nb % SAMPLE_GROUP == 0 else 1
    row = lambda b, pt: (b, 0)
    rt = group * ts
    grid_spec = pltpu.PrefetchScalarGridSpec(
        num_scalar_prefetch=1,
        grid=(nb // group,),
        in_specs=[pl.BlockSpec((rt, 256), row),
                  pl.BlockSpec((rt, LANES), row),
                  pl.BlockSpec((rt, IDX_DIM), row)] + _page_specs((IDX_DIM, page), layer, n_pages,
                                                                  group),
        out_specs=pl.BlockSpec((rt, lp), row),
    )
    return pl.pallas_call(
        functools.partial(_sample_scores_kernel, n_pages=n_pages, page=page, ts=ts, group=group),
        grid_spec=grid_spec,
        out_shape=jax.ShapeDtypeStruct((nb * ts, lp), F32),
        compiler_params=pltpu.CompilerParams(dimension_semantics=("arbitrary",),
                                             vmem_limit_bytes=VMEM_LIMIT),
        name="sample_scores",
    )(page_table, iq, iw, ik_new, *([cache_idx] * (group * n_pages)))


def _sample_select_kernel(sc_in_ref, tril_ref, bias_ref, sc_ref, sb_ref, *, topk, lp):
    nch = lp // LANES
    for c in range(nch):
        rows = slice(c * LANES, (c + 1) * LANES)
        t = sc_in_ref[:, rows].T
        sc_ref[rows, :] = t
        sb_ref[rows, :] = t.astype(BF16)
    tau, need = _kth_largest_t(sc_ref, sb_ref, nch, LANES, topk)
    need = jnp.where(tau == -jnp.inf, 0.0, need)
    tril = tril_ref[...]
    tie_carry = jnp.zeros((8, LANES), F32)
    for c in range(nch):
        rows = slice(c * LANES, (c + 1) * LANES)
        bias, tie_carry = _select_bias_t(_rows8(sc_ref[rows, :]), tau, need, tie_carry, tril, LANES)
        bias_ref[:, rows] = bias.reshape(LANES, LANES).T


def _sample_select(scores, tril, topk):
    n, lp = scores.shape
    rows = LANES
    return pl.pallas_call(
        functools.partial(_sample_select_kernel, topk=topk, lp=lp),
        grid=(n // rows,),
        in_specs=[pl.BlockSpec((rows, lp), lambda i: (i, 0)),
                  pl.BlockSpec((LANES, LANES), lambda i: (0, 0))],
        out_specs=pl.BlockSpec((rows, lp), lambda i: (i, 0)),
        out_shape=jax.ShapeDtypeStruct((n, lp), F32),
        scratch_shapes=[pltpu.VMEM((lp, LANES), F32), pltpu.VMEM((lp, LANES), BF16)],
        compiler_params=pltpu.CompilerParams(dimension_semantics=("arbitrary",),
                                             vmem_limit_bytes=VMEM_LIMIT),
        name="sample_select",
    )(scores, tril)


def _sample_attn_kernel(pt_ref, aq_ref, az_ref, bias_ref, knew_ref, vnew_ref, *rest,
                        n_pages, page, ts, group):
    del pt_ref
    nk = group * n_pages
    k_refs, v_refs, ga_ref = rest[:nk], rest[nk:2 * nk], rest[2 * nk]
    past = n_pages * page
    pad = jnp.zeros((page - ts, LANES), F32)
    lane = lax.broadcasted_iota(I32, (ts, LANES), 1)

    for g in range(group):
        rows = slice(g * ts, (g + 1) * ts)
        aq = aq_ref[rows, :].astype(F32) * (ATT_HEAD_DIM ** -0.5)
        lo_heads, hi_heads = [], []
        for s in range(4):
            lo, hi = _split_heads(aq[:, s * LANES:(s + 1) * LANES], ts)
            lo_heads.append(lo)
            hi_heads.append(hi)
        qs = jnp.concatenate(lo_heads + hi_heads, axis=0).astype(BF16)
        k_new = jnp.concatenate([knew_ref[rows, :], pad], axis=0).astype(BF16)
        v_new = jnp.concatenate([vnew_ref[rows, :], pad], axis=0).astype(BF16)
        pages = slice(g * n_pages, (g + 1) * n_pages)
        kt = jnp.concatenate([r[...].astype(BF16) for r in k_refs[pages]], axis=1)
        vt = jnp.concatenate([r[...].astype(BF16) for r in v_refs[pages]], axis=1)
        s = jnp.concatenate([jnp.dot(qs, kt, preferred_element_type=F32),
                             lax.dot_general(qs, k_new, NT_DIMS, preferred_element_type=F32)],
                            axis=1)
        s = s + jnp.concatenate([bias_ref[rows, :]] * ATT_HEADS, axis=0)
        p = jnp.exp(s - jnp.max(s, axis=1, keepdims=True))
        denom = jnp.sum(p, axis=1, keepdims=True)
        pb = p.astype(BF16)
        acc = (lax.dot_general(pb[:, :past], vt, NT_DIMS, preferred_element_type=F32)
               + jnp.dot(pb[:, past:], v_new, preferred_element_type=F32))
        o = acc / denom
        for s in range(4):
            oo = jnp.where(lane < ATT_HEAD_DIM, o[s * ts:(s + 1) * ts], o[(s + 4) * ts:(s + 5) * ts])
            az = az_ref[rows, s * LANES:(s + 1) * LANES].astype(F32)
            ga_ref[rows, s * LANES:(s + 1) * LANES] = (oo * _silu(az)).astype(ga_ref.dtype)


def _page_specs(shape, layer, n_pages, group):
    return [pl.BlockSpec((None, None) + shape,
                         functools.partial(lambda b, pt, g, p: (layer, pt[b * group + g, p], 0, 0),
                                           g=g, p=p))
            for g in range(group) for p in range(n_pages)]


def _sample_attn(page_table, aq, az, bias, k_new, v_new, cache_k, cache_v, layer, ts):
    nb, n_pages = page_table.shape
    page = cache_k.shape[3]
    lp = (n_pages + 1) * page
    group = SAMPLE_GROUP if nb % SAMPLE_GROUP == 0 else 1
    row = lambda b, pt: (b, 0)
    page_specs = _page_specs((LANES, page), layer, n_pages, group)
    rt = group * ts
    grid_spec = pltpu.PrefetchScalarGridSpec(
        num_scalar_prefetch=1,
        grid=(nb // group,),
        in_specs=[pl.BlockSpec((rt, 512), row),
                  pl.BlockSpec((rt, 512), row),
                  pl.BlockSpec((rt, lp), row),
                  pl.BlockSpec((rt, LANES), row),
                  pl.BlockSpec((rt, LANES), row)] + page_specs + page_specs,
        out_specs=pl.BlockSpec((rt, 512), row),
    )
    return pl.pallas_call(
        functools.partial(_sample_attn_kernel, n_pages=n_pages, page=page, ts=ts, group=group),
        grid_spec=grid_spec,
        out_shape=jax.ShapeDtypeStruct((nb * ts, 512), aq.dtype),
        compiler_params=pltpu.CompilerParams(dimension_semantics=("arbitrary",),
                                             vmem_limit_bytes=VMEM_LIMIT),
        name="sample_attn",
    )(page_table, aq, az, bias, k_new, v_new, *([cache_k] * (group * n_pages)),
      *([cache_v] * (group * n_pages)))


def _merge_kernel(x_ref, p_ref, gr_ref, ga_ref, br_ref, ba_ref, wor_ref, woa_ref, wout_ref,
                  wpg_ref, wpp_ref, y_ref):
    u_r = jnp.dot(br_ref[...].astype(BF16), wor_ref[...], preferred_element_type=F32)
    u_a = jnp.dot(ba_ref[...].astype(BF16), woa_ref[...], preferred_element_type=F32)
    m = (jax.nn.sigmoid(gr_ref[...].astype(F32)) * u_r
         + jax.nn.sigmoid(ga_ref[...].astype(F32)) * u_a)
    x1 = x_ref[...] + jnp.dot(m.astype(BF16), wout_ref[...], preferred_element_type=F32)
    gate = jax.nn.sigmoid(jnp.dot(x1.astype(BF16), wpg_ref[...], preferred_element_type=F32))
    y_ref[...] = x1 + gate * jnp.dot(p_ref[...].astype(BF16), wpp_ref[...],
                                     preferred_element_type=F32)


def _merge(x, p_all, layer, gr, ga, br, ba, weights, tm):
    n, d = x.shape
    row = lambda i: (i, 0)
    wspec = lambda w: pl.BlockSpec((None,) + w.shape[1:], lambda i: (layer, 0, 0))
    return pl.pallas_call(
        _merge_kernel,
        grid=(n // tm,),
        in_specs=[pl.BlockSpec((tm, d), row),
                  pl.BlockSpec((None, tm, p_all.shape[2]), lambda i: (layer, i, 0)),
                  pl.BlockSpec((tm, d), row),
                  pl.BlockSpec((tm, d), row),
                  pl.BlockSpec((tm, 512), row),
                  pl.BlockSpec((tm, 512), row)] + [wspec(w) for w in weights],
        out_specs=pl.BlockSpec((tm, d), row),
        out_shape=jax.ShapeDtypeStruct((n, d), F32),
        compiler_params=pltpu.CompilerParams(dimension_semantics=("arbitrary",),
                                             vmem_limit_bytes=VMEM_LIMIT),
        name="merge",
    )(x, p_all, gr, ga, br, ba, *weights)


def _pair_heads_rows(m):
    d = m.shape[1]
    m = m.reshape(ATT_KV_HEADS, ATT_HEADS // ATT_KV_HEADS, ATT_HEAD_DIM, d)
    return jnp.concatenate([m[0], m[1]], axis=1).reshape(ATT_HEADS * ATT_HEAD_DIM, d)


def _pack_w_in(wt):
    sizes = (512, 512, 512, 512, 512, 128, 128, 512, 256, 64, 4, 1024, 1024)
    offs, o = [], 0
    for s in sizes:
        offs.append((o, o + s))
        o += s
    rq, rk, rv, rz, aq, ak, av, az, iq, ik, iw, gr, ga = [wt[a:b] for a, b in offs]
    iwp = jnp.pad(iw, ((0, LANES - IDX_HEADS), (0, 0)))
    packed = jnp.concatenate([rq, rk, rv, rz, _pair_heads_rows(aq), _pair_heads_rows(az), ak, av,
                              iq, ik, ik, iwp, gr, ga], axis=0)
    return packed.astype(BF16)


def _rope_tables(pos):
    pf = pos.astype(F32)[:, None]
    lane = jnp.arange(LANES)[None, :]
    half = RET_DK // 2
    freqs = jnp.exp(-math.log(RET_THETA) * jnp.arange(half, dtype=F32) / half)
    ang = pf * jnp.tile(freqs, LANES // half)[None, :]
    rc = jnp.cos(ang)
    rs = jnp.where(lane < half, -jnp.sin(ang), jnp.sin(ang))
    half = ROPE_DIM // 2
    freqs = jnp.exp(-math.log(ROPE_THETA) * jnp.arange(half, dtype=F32) / half)
    ang = pf * jnp.tile(freqs, LANES // half)[None, :]
    cos, sin = jnp.cos(ang), jnp.sin(ang)
    d = lane % ATT_HEAD_DIM
    ac = jnp.where(d < ROPE_DIM, cos, 1.0)
    as1 = jnp.where(d < half, -sin, 0.0)
    as2 = jnp.where((d >= half) & (d < ROPE_DIM), sin, 0.0)
    return rc, rs, ac, as1, as2


def _decay_tables(c_eff, cpad):
    h = RET_HEADS
    log_g = jnp.log1p(-jnp.exp2(-5.0 - jnp.arange(h, dtype=F32)))
    c = jnp.arange(cpad, dtype=F32)
    diff = c[:, None] - c[None, :]
    di = jnp.where(diff[None] >= 0, jnp.exp(jnp.maximum(diff, 0.0)[None] * log_g[:, None, None]), 0.0)
    dq = jnp.exp((c[:, None] + 1.0) * log_g[None, :])
    dk = jnp.exp((c_eff - 1.0 - c)[:, None] * log_g[None, :])
    dk = jnp.where(c[:, None] < c_eff, dk, 0.0)
    ds = jnp.exp(c_eff * log_g)
    rep = lambda a: jnp.repeat(a, RET_DK, axis=-1)
    return di, rep(dq), rep(dk), rep(ds[None, :])


def _tril(n):
    r = lax.broadcasted_iota(I32, (n, n), 0)
    c = lax.broadcasted_iota(I32, (n, n), 1)
    return jnp.where(c <= r, 1.0, 0.0).astype(BF16)


def _group_sum_matrix():
    r = lax.broadcasted_iota(I32, (LANES, LANES), 0) // ATT_HEAD_DIM
    c = lax.broadcasted_iota(I32, (LANES, LANES), 1) // ATT_HEAD_DIM
    return jnp.where(r == c, 1.0, 0.0).astype(BF16)


def _layer_weights(i, norm_gain, w_in_t, q_norm_gain, k_norm_gain):
    return dict(
        gain=norm_gain[i][None, :],
        w_in=_pack_w_in(w_in_t[:, i, :]),
        qg=jnp.tile(q_norm_gain[i], 2)[None, :],
        kg=jnp.tile(k_norm_gain[i], 2)[None, :],
    )


def _merge_weights(w_o_ret, w_o_att, w_out, w_ple_gate, w_ple_proj):
    depth, _, d = w_o_att.shape
    woa = w_o_att.reshape(depth, ATT_KV_HEADS, ATT_HEADS // ATT_KV_HEADS, ATT_HEAD_DIM, d)
    woa = jnp.concatenate([woa[:, 0], woa[:, 1]], axis=2).reshape(depth, -1, d)
    return tuple(w.astype(BF16) for w in (w_o_ret, woa, w_out, w_ple_gate, w_ple_proj))


def kernel(x_prompt, x_sample, cache_k, cache_v, cache_idx_k, state_ret, page_table, p_prompt,
           p_sample, norm_gain, w_in, q_norm_gain, k_norm_gain, w_o_ret, w_o_att, w_out, w_ple_gate,
           w_ple_proj):
    bp, tp, d = x_prompt.shape
    bs, ts, _ = x_sample.shape
    depth = w_in.shape[0]
    n_pool, page = cache_k.shape[1], cache_k.shape[2]
    n_pages = page_table.shape[1]
    past = n_pages * page
    topk_p = min(TOPK_MAX, tp // 4)
    topk_s = min(TOPK_MAX, (past + ts) // 4)
    assert tp % KEY_CHUNK == 0 and tp % RET_BLOCK == 0 and ts <= page and ts % 8 == 0
    assert topk_p <= KEY_CHUNK

    np_, ns_ = bp * tp, bs * ts
    tm_p = 512 if np_ % 512 == 0 else Q_TILE
    tm_s = 256 if ns_ % 256 == 0 else ns_
    assert tm_s % ts == 0 and tp % tm_p == 0

    gsum = _group_sum_matrix()
    tabs_p = _rope_tables(jnp.arange(tp, dtype=I32))
    tabs_s = _rope_tables(past + (jnp.arange(tm_s, dtype=I32) % ts))
    decay_p = _decay_tables(float(RET_BLOCK), RET_BLOCK)
    decay_s = _decay_tables(float(ts), DEC_PAD)
    tril_p = _tril(LANES)
    ck = cache_k.reshape(depth, n_pool, page, LANES).transpose(0, 1, 3, 2)
    cv = cache_v.reshape(depth, n_pool, page, LANES).transpose(0, 1, 3, 2)
    cik = cache_idx_k.transpose(0, 1, 3, 2)
    w_in_t = w_in.transpose(2, 0, 1)
    s0_p = jnp.zeros((1, bp, RET_HEADS, RET_DK, RET_DV), F32)

    xp = x_prompt.reshape(np_, d)
    xs = x_sample.reshape(ns_, d)
    outs = {k: [] for k in ("kp", "vp", "ikp", "sp", "ks", "vs", "iks", "ss")}

    pp_all = p_prompt.reshape(depth, np_, -1)
    ps_all = p_sample.reshape(depth, ns_, -1)
    mw = _merge_weights(w_o_ret, w_o_att, w_out, w_ple_gate, w_ple_proj)

    for i in range(depth):
        lw = _layer_weights(i, norm_gain, w_in_t, q_norm_gain, k_norm_gain)

        (rq, rk, rv, rz, aq, az, akb, avt, iq, ikk, gr, ga, ak, av, ik, iw) = _inproj(
            xp, lw["gain"], lw["w_in"], tabs_p, lw["qg"], lw["kg"], gsum, tm_p, tp // tm_p, BF16)
        seq3 = lambda a: a.reshape(bp, tp, 512)
        b_r, s_new = _retention(seq3(rq), seq3(rk), seq3(rv), seq3(rz), s0_p, 0, decay_p, bp,
                                tp // RET_BLOCK, RET_BLOCK, BF16)
        b_r = b_r.reshape(np_, 512)
        b_a = _attn_prompt(aq, iq, iw, az, akb, avt, ikk, tril_p, bp, tp, topk_p)
        xp = _merge(xp, pp_all, i, gr, ga, b_r, b_a, mw, tm_p)
        outs["kp"].append(ak.reshape(bp, tp, ATT_KV_HEADS, ATT_HEAD_DIM))
        outs["vp"].append(av.reshape(bp, tp, ATT_KV_HEADS, ATT_HEAD_DIM))
        outs["ikp"].append(ik.reshape(bp, tp, IDX_DIM))
        outs["sp"].append(s_new)

        (rq, rk, rv, rz, aq, az, akb, avb, iq, ikk, gr, ga, ak, av, ik, iw) = _inproj(
            xs, lw["gain"], lw["w_in"], tabs_s, lw["qg"], lw["kg"], gsum, tm_s, 1, F32)
        tok3 = lambda a: a.reshape(bs, ts, 512)
        b_r, s_new = _retention(tok3(rq), tok3(rk), tok3(rv), tok3(rz), state_ret, i, decay_s, bs,
                                1, ts, F32)
        b_r = b_r.reshape(ns_, 512)
        keys = _sample_scores(page_table, iq, iw, ik, cik, i, ts)
        bias = _sample_select(keys, tril_p, topk_s)
        b_a = _sample_attn(page_table, aq, az, bias, ak, av, ck, cv, i, ts)
        xs = _merge(xs, ps_all, i, gr, ga, b_r, b_a, mw, tm_s)
        outs["ks"].append(ak.reshape(bs, ts, ATT_KV_HEADS, ATT_HEAD_DIM))
        outs["vs"].append(av.reshape(bs, ts, ATT_KV_HEADS, ATT_HEAD_DIM))
        outs["iks"].append(ik.reshape(bs, ts, IDX_DIM))
        outs["ss"].append(s_new)

    st = lambda k: jnp.stack(outs[k])
    return (xp.reshape(bp, tp, d), xs.reshape(bs, ts, d), st("kp"), st("vp"), st("ikp"), st("sp"),
            st("ks"), st("vs"), st("iks"), st("ss"))
```
